```python
import jax, jax.numpy as jnp
from jax import lax
import numpy as np

D_MODEL = 1024
BATCH = 2
SEQ = 16384
DEPTH = 1
DEC_BATCH = 32
DEC_SEQ = 64
PAST_LEN = 2048

CHUNK = 64
H_A = 4
DK_A = 128
DV_A = 128
K_A = H_A * DK_A
V_A = H_A * DV_A
H_B = 8
DH_B = 64
W_B = H_B * DH_B
N_PAST_CHUNKS = 8
WINDOW = N_PAST_CHUNKS * CHUNK
REL_CLIP = 128
NUM_REL = CHUNK + REL_CLIP
D_FF = -(-8 * D_MODEL // (3 * 256)) * 256
SPLIT_SIZES = [K_A, K_A, V_A, V_A, W_B, W_B, W_B, D_MODEL, D_MODEL]
IN_COLS = int(sum(SPLIT_SIZES))
EPS = 1e-6
NEG = -1e30

kernel_name = 'hgrn2_chunkband_attn_hybrid_step'


def rmsnorm(x, g):
    xf = x.astype(jnp.float32)
    y = xf * lax.rsqrt(jnp.mean(xf * xf, axis=-1, keepdims=True) + EPS) * g.astype(jnp.float32)
    return y.astype(x.dtype)


def hgrn2_recurrence(q, logf, k, v, s0):
    b, l = q.shape[:2]
    c = CHUNK if l % CHUNK == 0 else l
    n = l // c

    def to_chunks(t):
        return t.reshape(b, n, c, *t.shape[2:]).swapaxes(0, 1)

    tri = jnp.tril(jnp.ones((c, c), dtype=bool))

    def step(s, inp):
        qc, lfc, kc, vc = inp
        cum = jnp.cumsum(lfc, axis=1)
        tot = cum[:, -1]
        ref = cum[:, c // 2][:, None]
        a = jnp.einsum('bthk,bshk->bhts', qc * jnp.exp(cum - ref), kc * jnp.exp(ref - cum))
        a = jnp.where(tri, a, 0.0)
        o = (jnp.einsum('bhts,bshv->bthv', a, vc)
             + jnp.einsum('bthk,bhkv->bthv', qc * jnp.exp(cum), s))
        s_new = (jnp.exp(tot)[..., None] * s
                 + jnp.einsum('bshk,bshv->bhkv', kc * jnp.exp(tot[:, None] - cum), vc))
        return s_new, o

    s_fin, o = lax.scan(step, s0, (to_chunks(q), to_chunks(logf), to_chunks(k), to_chunks(v)))
    o = o.swapaxes(0, 1).reshape(b, l, *o.shape[3:])
    return o, s_fin


def band_attention(q, k, v, q_pos, k_pos, rel_bias):
    s = jnp.einsum('bqhd,bkhd->bhqk', q, k).astype(jnp.float32) * (DH_B ** -0.5)
    dist = q_pos[:, None] - k_pos[None, :]
    bias = rel_bias.astype(jnp.float32)[:, jnp.clip(dist, -(CHUNK - 1), REL_CLIP) + (CHUNK - 1)]
    qc = (q_pos // CHUNK)[:, None]
    kc = (k_pos // CHUNK)[None, :]
    mask = (k_pos[None, :] >= 0) & (kc <= qc) & (qc - kc <= N_PAST_CHUNKS)
    s = jnp.where(mask[None, None], s + bias[None], NEG)
    p = jax.nn.softmax(s, axis=-1).astype(v.dtype)
    return jnp.einsum('bhqk,bkhd->bqhd', p, v)


def prompt_band_attention(q, k, v, rel_bias):
    b, l, h, d = q.shape
    n = l // CHUNK
    pad = ((0, 0), (WINDOW, 0), (0, 0), (0, 0))
    kp = jnp.pad(k, pad)
    vp = jnp.pad(v, pad)

    def one_chunk(i):
        start = i * CHUNK
        qc = lax.dynamic_slice_in_dim(q, start, CHUNK, axis=1)
        kc = lax.dynamic_slice_in_dim(kp, start, WINDOW + CHUNK, axis=1)
        vc = lax.dynamic_slice_in_dim(vp, start, WINDOW + CHUNK, axis=1)
        q_pos = start + jnp.arange(CHUNK, dtype=jnp.int32)
        k_pos = start - WINDOW + jnp.arange(WINDOW + CHUNK, dtype=jnp.int32)
        return band_attention(qc, kc, vc, q_pos, k_pos, rel_bias)

    o = lax.map(one_chunk, jnp.arange(n, dtype=jnp.int32))
    return o.swapaxes(0, 1).reshape(b, l, h, d)


def mixer(h, lb, w_in, hgrn_out_norm, w_branch_a, rel_bias, w_branch_b, w_out, s0, kv_cache):
    b, l, _ = h.shape
    p = h @ w_in
    qa, fa, ia, ga, qb, kb, vb, gate_a, gate_b = jnp.split(
        p, list(np.cumsum(SPLIT_SIZES)[:-1]), axis=-1)
    f = lb + (1.0 - lb) * jax.nn.sigmoid(fa.astype(jnp.float32))
    logf = jnp.log(f).reshape(b, l, H_A, DK_A)
    k_a = (1.0 - f).reshape(b, l, H_A, DK_A)
    q_a = jax.nn.silu(qa.astype(jnp.float32)).reshape(b, l, H_A, DK_A)
    v_a = ia.astype(jnp.float32).reshape(b, l, H_A, DV_A)
    o_a, s_fin = hgrn2_recurrence(q_a, logf, k_a, v_a, s0.astype(jnp.float32))
    o_a = o_a * lax.rsqrt(jnp.mean(o_a * o_a, axis=-1, keepdims=True) + EPS)
    o_a = (o_a.reshape(b, l, V_A) * hgrn_out_norm.astype(jnp.float32)).astype(h.dtype)
    y_a = (o_a * jax.nn.silu(ga)) @ w_branch_a
    q_b = qb.reshape(b, l, H_B, DH_B)
    k_b = kb.reshape(b, l, H_B, DH_B)
    v_b = vb.reshape(b, l, H_B, DH_B)
    if kv_cache is None:
        o_b = prompt_band_attention(q_b, k_b, v_b, rel_bias)
        keep = min(WINDOW, l)
        k_rows, v_rows = k_b[:, l - keep:], v_b[:, l - keep:]
    else:
        ck, cv = kv_cache
        w = ck.shape[1]
        q_pos = PAST_LEN + jnp.arange(l, dtype=jnp.int32)
        k_pos = jnp.concatenate([PAST_LEN - w + jnp.arange(w, dtype=jnp.int32), q_pos])
        o_b = band_attention(q_b, jnp.concatenate([ck, k_b], axis=1),
                             jnp.concatenate([cv, v_b], axis=1), q_pos, k_pos, rel_bias)
        k_rows, v_rows = k_b, v_b
    y_b = o_b.reshape(b, l, W_B) @ w_branch_b
    y = (jax.nn.sigmoid(gate_a) * y_a + jax.nn.sigmoid(gate_b) * y_b) @ w_out
    return y, s_fin.astype(h.dtype), k_rows, v_rows


def trunk(x, c, states, caches_k, caches_v, w_ada, b_ada, norm_mix, w_in, hgrn_lb_logits,
          hgrn_out_norm, w_branch_a, rel_bias, w_branch_b, w_out, norm_ffn, w_ffn_in,
          w_ffn_out, norm_final):
    b = x.shape[0]
    lb_all = jnp.cumsum(jax.nn.softmax(hgrn_lb_logits.astype(jnp.float32), axis=0), axis=0)
    new_s, new_k, new_v = [], [], []
    for layer in range(DEPTH):
        mod = jax.nn.silu(c) @ w_ada[layer] + b_ada[layer]
        sh1, sc1, g1, sh2, sc2, g2 = jnp.split(mod[:, None, :], 6, axis=-1)
        h = rmsnorm(x, norm_mix[layer]) * (1.0 + sc1) + sh1
        if states is None:
            s0 = jnp.zeros((b, H_A, DK_A, DV_A), jnp.float32)
            kv = None
        else:
            s0 = states[layer]
            kv = (caches_k[layer], caches_v[layer])
        y, s_fin, k_rows, v_rows = mixer(h, lb_all[layer], w_in[layer], hgrn_out_norm[layer],
                                         w_branch_a[layer], rel_bias[layer], w_branch_b[layer],
                                         w_out[layer], s0, kv)
        x = x + g1 * y
        h = rmsnorm(x, norm_ffn[layer]) * (1.0 + sc2) + sh2
        a, u = jnp.split(h @ w_ffn_in[layer], 2, axis=-1)
        x = x + g2 * ((jax.nn.silu(a) * u) @ w_ffn_out[layer])
        new_s.append(s_fin)
        new_k.append(k_rows)
        new_v.append(v_rows)
    return rmsnorm(x, norm_final), jnp.stack(new_s), jnp.stack(new_k), jnp.stack(new_v)


def setup_inputs(seed: int = 0) -> dict:
    key = jax.random.key(seed)
    ks = jax.random.split(key, 24)
    f32 = jnp.float32

    def nrm(k, shape, scale=1.0):
        return jax.random.normal(k, shape, f32) * scale

    cache_rows = min(WINDOW, PAST_LEN)
    return {
        'x_prompt': nrm(ks[0], (BATCH, SEQ, D_MODEL)),
        'x_sample': nrm(ks[1], (DEC_BATCH, DEC_SEQ, D_MODEL)),
        'c_prompt': nrm(ks[2], (BATCH, D_MODEL)),
        'c_sample': nrm(ks[3], (DEC_BATCH, D_MODEL)),
        'state_hgrn': nrm(ks[4], (DEPTH, DEC_BATCH, H_A, DK_A, DV_A), 0.5),
        'cache_k': nrm(ks[5], (DEPTH, DEC_BATCH, cache_rows, H_B, DH_B)),
        'cache_v': nrm(ks[6], (DEPTH, DEC_BATCH, cache_rows, H_B, DH_B)),
        'w_ada': nrm(ks[7], (DEPTH, D_MODEL, 6 * D_MODEL), 0.5 * D_MODEL ** -0.5),
        'b_ada': nrm(ks[8], (DEPTH, 6 * D_MODEL), 0.02),
        'norm_mix': 1.0 + nrm(ks[9], (DEPTH, D_MODEL), 0.05),
        'w_in': nrm(ks[10], (DEPTH, D_MODEL, IN_COLS), D_MODEL ** -0.5),
        'hgrn_lb_logits': nrm(ks[11], (DEPTH + 1, K_A), 0.1),
        'hgrn_out_norm': 1.0 + nrm(ks[12], (DEPTH, V_A), 0.05),
        'w_branch_a': nrm(ks[13], (DEPTH, V_A, D_MODEL), V_A ** -0.5),
        'rel_bias': nrm(ks[14], (DEPTH, H_B, NUM_REL), 0.5),
        'w_branch_b': nrm(ks[15], (DEPTH, W_B, D_MODEL), W_B ** -0.5),
        'w_out': nrm(ks[16], (DEPTH, D_MODEL, D_MODEL), D_MODEL ** -0.5),
        'norm_ffn': 1.0 + nrm(ks[17], (DEPTH, D_MODEL), 0.05),
        'w_ffn_in': nrm(ks[18], (DEPTH, D_MODEL, 2 * D_FF), D_MODEL ** -0.5),
        'w_ffn_out': nrm(ks[19], (DEPTH, D_FF, D_MODEL), D_FF ** -0.5),
        'norm_final': 1.0 + nrm(ks[20], (D_MODEL,), 0.05),
    }


def reference(x_prompt, x_sample, c_prompt, c_sample, state_hgrn, cache_k, cache_v, w_ada, b_ada,
              norm_mix, w_in, hgrn_lb_logits, hgrn_out_norm, w_branch_a, rel_bias, w_branch_b,
              w_out, norm_ffn, w_ffn_in, w_ffn_out, norm_final):
    y_prompt, s_p, k_p, v_p = trunk(x_prompt, c_prompt, None, None, None, w_ada, b_ada, norm_mix,
                                    w_in, hgrn_lb_logits, hgrn_out_norm, w_branch_a, rel_bias,
                                    w_branch_b, w_out, norm_ffn, w_ffn_in, w_ffn_out, norm_final)
    y_sample, s_s, k_s, v_s = trunk(x_sample, c_sample, state_hgrn, cache_k, cache_v, w_ada, b_ada,
                                    norm_mix, w_in, hgrn_lb_logits, hgrn_out_norm, w_branch_a,
                                    rel_bias, w_branch_b, w_out, norm_ffn, w_ffn_in, w_ffn_out,
                                    norm_final)
    return (y_prompt, y_sample, s_p, k_p, v_p, s_s, k_s, v_s)
```

```python
import functools

import jax
import jax.numpy as jnp
from jax import lax
from jax.experimental import pallas as pl
from jax.experimental.pallas import tpu as pltpu

D_MODEL = 1024
CHUNK = 64
H_A, DK_A, DV_A = 4, 128, 128
K_A = H_A * DK_A
V_A = H_A * DV_A
H_B, DH_B = 8, 64
W_B = H_B * DH_B
N_PAST_CHUNKS = 8
WINDOW = N_PAST_CHUNKS * CHUNK
BAND = WINDOW + CHUNK
REL_CLIP = 128
NUM_REL = CHUNK + REL_CLIP
D_FF = 2816
IN_COLS = 4 * K_A + 3 * W_B + 2 * D_MODEL
EPS = 1e-6
NEG = -1e30

F32 = jnp.float32
BF16 = jnp.bfloat16

_OFF_QA, _OFF_FA, _OFF_IA, _OFF_GA = 0, K_A, 2 * K_A, 2 * K_A + V_A
_OFF_QB = 2 * K_A + 2 * V_A
_OFF_KB = _OFF_QB + W_B
_OFF_VB = _OFF_KB + W_B
_OFF_GATE_A = _OFF_VB + W_B
_OFF_GATE_B = _OFF_GATE_A + D_MODEL

_VMEM_LIMIT_BYTES = 56 * 1024 * 1024
_FFN_CHUNK = D_FF // 2


def _params(*sem):
    return pltpu.CompilerParams(dimension_semantics=sem, vmem_limit_bytes=_VMEM_LIMIT_BYTES)


def _resident(shape):
    zeros = (0,) * len(shape)
    return pl.BlockSpec(shape, lambda *_: zeros, pipeline_mode=pl.Buffered(1))


def _silu(x):
    return x * jax.nn.sigmoid(x)


def _rms(x, g):
    return x * lax.rsqrt(jnp.mean(x * x, axis=-1, keepdims=True) + EPS) * g


def _mod_kernel(c_ref, w_ref, b_ref, o_ref):
    s = _silu(c_ref[...])
    o_ref[...] = jnp.dot(s, w_ref[...], precision=lax.Precision.HIGHEST,
                         preferred_element_type=F32) + b_ref[...]


def _modulation(c, w_ada, b_ada):
    b = c.shape[0]
    n_out = w_ada.shape[1]
    return pl.pallas_call(
        _mod_kernel,
        out_shape=jax.ShapeDtypeStruct((b, n_out), F32),
        grid=(n_out // D_MODEL,),
        in_specs=[
            pl.BlockSpec((b, D_MODEL), lambda j: (0, 0)),
            pl.BlockSpec((D_MODEL, D_MODEL), lambda j: (0, j)),
            pl.BlockSpec((1, D_MODEL), lambda j: (0, j)),
        ],
        out_specs=pl.BlockSpec((b, D_MODEL), lambda j: (0, j)),
        compiler_params=_params("arbitrary"),
        name="adaln_mod",
    )(c, w_ada, b_ada.reshape(1, n_out))


def _bias_kernel(rb_ref, o_ref):
    i = pl.program_id(0)
    key = lax.broadcasted_iota(jnp.int32, (NUM_REL, BAND), 1)
    slot = lax.broadcasted_iota(jnp.int32, (NUM_REL, BAND), 0)
    idx = jnp.clip(i - key + WINDOW, -(CHUNK - 1), REL_CLIP) + (CHUNK - 1)
    onehot = jnp.where(slot == idx, 1.0, 0.0).astype(F32)
    o_ref[0] = jnp.dot(rb_ref[...], onehot, precision=lax.Precision.HIGHEST,
                       preferred_element_type=F32)


def _bias_tile(rel_bias):
    out = pl.pallas_call(
        _bias_kernel,
        out_shape=jax.ShapeDtypeStruct((CHUNK, H_B, BAND), F32),
        grid=(CHUNK,),
        in_specs=[pl.BlockSpec((H_B, NUM_REL), lambda i: (0, 0))],
        out_specs=pl.BlockSpec((1, H_B, BAND), lambda i: (i, 0, 0)),
        compiler_params=_params("arbitrary"),
        name="rel_bias_tile",
    )(rel_bias)
    return out.transpose(1, 0, 2)


def _inproj_kernel(x_ref, sc_ref, sh_ref, g_ref, lbl_ref, w_ref,
                   qa_ref, lf_ref, ka_ref, va_ref, gs_ref, qb_ref, kb_ref, vb_ref,
                   gta_ref, gtb_ref):
    bb, tl, d = x_ref.shape
    x = x_ref[...]
    h = _rms(x, g_ref[...]) * (1.0 + sc_ref[...]) + sh_ref[...]
    hb = h.reshape(bb * tl, d).astype(BF16)

    def proj(off, width):
        return jnp.dot(hb, w_ref[:, off:off + width], preferred_element_type=F32)

    def put(ref, val):
        ref[...] = val.reshape(ref.shape)

    lbl = lbl_ref[...]
    e = jnp.exp(lbl - jnp.max(lbl, axis=0, keepdims=True))
    lb = e[0:1] / jnp.sum(e, axis=0, keepdims=True)

    put(qa_ref, _silu(proj(_OFF_QA, K_A)))
    f = lb + (1.0 - lb) * jax.nn.sigmoid(proj(_OFF_FA, K_A))
    put(lf_ref, jnp.log(f))
    put(ka_ref, 1.0 - f)
    put(va_ref, proj(_OFF_IA, V_A))
    put(gs_ref, _silu(proj(_OFF_GA, V_A)))
    put(qb_ref, proj(_OFF_QB, W_B) * (DH_B ** -0.5))
    put(kb_ref, proj(_OFF_KB, W_B))
    put(vb_ref, proj(_OFF_VB, W_B))
    put(gta_ref, jax.nn.sigmoid(proj(_OFF_GATE_A, D_MODEL)))
    put(gtb_ref, jax.nn.sigmoid(proj(_OFF_GATE_B, D_MODEL)))


def _inproj(x, sc, sh, norm_g, lb_logits, w_in_bf16, bb, tl):
    b, l, d = x.shape
    widths = [K_A, K_A, K_A, V_A, V_A, W_B, W_B, W_B, D_MODEL, D_MODEL]
    tok = lambda w: pl.BlockSpec((bb, tl, w), lambda i, j: (i, j, 0))
    per_b = pl.BlockSpec((bb, 1, d), lambda i, j: (i, 0, 0))
    return pl.pallas_call(
        _inproj_kernel,
        out_shape=[jax.ShapeDtypeStruct((b, l, w), F32) for w in widths],
        grid=(b // bb, l // tl),
        in_specs=[tok(d), per_b, per_b, _resident((1, d)), _resident(lb_logits.shape),
                  _resident(w_in_bf16.shape)],
        out_specs=[tok(w) for w in widths],
        compiler_params=_params("arbitrary", "arbitrary"),
        name="norm_inproj",
    )(x, sc, sh, norm_g, lb_logits, w_in_bf16)


def _split3(x):
    hi = x.astype(BF16)
    r = x - hi.astype(F32)
    mid = r.astype(BF16)
    lo = (r - mid.astype(F32)).astype(BF16)
    return hi, mid, lo


def _hgrn_kernel(qa_ref, lf_ref, ka_ref, va_ref, gs_ref, og_ref, s0_ref,
                 o_ref, sfin_ref, st_ref):
    j = pl.program_id(1)
    n_chunks = qa_ref.shape[0] // CHUNK

    @pl.when(j == 0)
    def _():
        for h in range(H_A):
            st_ref[h] = s0_ref[h].T

    row = lax.broadcasted_iota(jnp.int32, (CHUNK, CHUNK), 0)
    col = lax.broadcasted_iota(jnp.int32, (CHUNK, CHUNK), 1)
    tri = row >= col
    tri_bf = jnp.where(tri, 1.0, 0.0).astype(BF16)
    nt = (((1,), (1,)), ((), ()))
    tn = (((0,), (0,)), ((), ()))

    for c in range(n_chunks):
        rows = slice(c * CHUNK, (c + 1) * CHUNK)
        cum = sum(jnp.dot(tri_bf, piece, preferred_element_type=F32)
                  for piece in _split3(lf_ref[rows, :]))
        for h in range(H_A):
            lanes = slice(h * DK_A, (h + 1) * DK_A)
            cum_h = cum[:, lanes]
            tot = cum_h[CHUNK - 1:CHUNK]
            mid = cum_h[CHUNK // 2:CHUNK // 2 + 1]
            q = qa_ref[rows, lanes]
            k = ka_ref[rows, lanes]
            v = va_ref[rows, lanes].astype(BF16)
            st = st_ref[h]
            a = lax.dot_general((q * jnp.exp(cum_h - mid)).astype(BF16),
                                (k * jnp.exp(mid - cum_h)).astype(BF16), nt,
                                preferred_element_type=F32)
            a = jnp.where(tri, a, 0.0).astype(BF16)
            o = (jnp.dot(a, v, preferred_element_type=F32)
                 + lax.dot_general((q * jnp.exp(cum_h)).astype(BF16), st.astype(BF16), nt,
                                   preferred_element_type=F32))
            st_ref[h] = (jnp.exp(tot) * st
                         + lax.dot_general(v, (k * jnp.exp(tot - cum_h)).astype(BF16), tn,
                                           preferred_element_type=F32))
            o = o * lax.rsqrt(jnp.mean(o * o, axis=-1, keepdims=True) + EPS)
            o_ref[rows, lanes] = o * og_ref[:, lanes] * gs_ref[rows, lanes]

    @pl.when(j == pl.num_programs(1) - 1)
    def _():
        for h in range(H_A):
            sfin_ref[h] = st_ref[h].T


def _hgrn(qa, lf, ka, va, gs, out_norm, s0, th):
    b, l, _ = qa.shape
    tok = pl.BlockSpec((None, th, K_A), lambda i, j: (i, j, 0))
    state = pl.BlockSpec((None, H_A, DK_A, DV_A), lambda i, j: (i, 0, 0, 0))
    return pl.pallas_call(
        _hgrn_kernel,
        out_shape=[jax.ShapeDtypeStruct((b, l, V_A), F32),
                   jax.ShapeDtypeStruct((b, H_A, DK_A, DV_A), F32)],
        grid=(b, l // th),
        in_specs=[tok, tok, tok, tok, tok, _resident((1, V_A)), state],
        out_specs=[tok, state],
        scratch_shapes=[pltpu.VMEM((H_A, DV_A, DK_A), F32)],
        compiler_params=_params("arbitrary", "arbitrary"),
        name="hgrn2_recurrence",
    )(qa, lf, ka, va, gs, out_norm, s0)


def _attn_kernel(q_ref, kp_ref, kc_ref, vp_ref, vc_ref, bias_ref, o_ref, kbuf, vbuf, *,
                 mask_first_tile):
    t = pl.program_id(1)
    tq = q_ref.shape[0]
    kbuf[0:WINDOW, :] = kp_ref[...].astype(BF16)
    kbuf[WINDOW:WINDOW + tq, :] = kc_ref[...].astype(BF16)
    vbuf[0:WINDOW, :] = vp_ref[...].astype(BF16)
    vbuf[WINDOW:WINDOW + tq, :] = vc_ref[...].astype(BF16)

    lane = lax.broadcasted_iota(jnp.int32, (CHUNK, 2 * DH_B), 1)
    low = lane < DH_B
    key = lax.broadcasted_iota(jnp.int32, (CHUNK, BAND), 1)
    nt = (((1,), (1,)), ((), ()))

    for c in range(tq // CHUNK):
        rows = slice(c * CHUNK, (c + 1) * CHUNK)
        band = slice(c * CHUNK, c * CHUNK + BAND)
        valid = (key + c * CHUNK >= WINDOW) | (t > 0) if mask_first_tile else None
        for g in range(H_B // 2):
            lanes = slice(g * 2 * DH_B, (g + 1) * 2 * DH_B)
            q2 = q_ref[rows, lanes]
            k2 = kbuf[band, lanes]
            v2 = vbuf[band, lanes]
            halves = []
            for half in range(2):
                qh = jnp.where(low if half == 0 else ~low, q2, 0.0).astype(BF16)
                s = lax.dot_general(qh, k2, nt, preferred_element_type=F32)
                s = s + bias_ref[2 * g + half]
                if valid is not None:
                    s = jnp.where(valid, s, NEG)
                p = jnp.exp(s - jnp.max(s, axis=-1, keepdims=True))
                denom = jnp.sum(p, axis=-1, keepdims=True)
                pv = jnp.dot(p.astype(BF16), v2, preferred_element_type=F32)
                halves.append(pv / denom)
            o_ref[rows, lanes] = jnp.where(low, halves[0], halves[1])


def _attention(q, k, v, k_prev, v_prev, bias, tq, mask_first_tile):
    b, l, _ = q.shape
    cur = pl.BlockSpec((None, tq, W_B), lambda i, j: (i, j, 0))
    if k_prev is None:
        assert tq == WINDOW
        k_prev, v_prev = k, v
        prev = pl.BlockSpec((None, WINDOW, W_B), lambda i, j: (i, jnp.maximum(j - 1, 0), 0))
    else:
        prev = pl.BlockSpec((None, WINDOW, W_B), lambda i, j: (i, 0, 0))
    return pl.pallas_call(
        functools.partial(_attn_kernel, mask_first_tile=mask_first_tile),
        out_shape=jax.ShapeDtypeStruct((b, l, W_B), F32),
        grid=(b, l // tq),
        in_specs=[cur, prev, cur, prev, cur, _resident(bias.shape)],
        out_specs=cur,
        scratch_shapes=[pltpu.VMEM((WINDOW + tq, W_B), BF16),
                        pltpu.VMEM((WINDOW + tq, W_B), BF16)],
        compiler_params=_params("arbitrary", "arbitrary"),
        name="band_attention",
    )(q, k_prev, k, v_prev, v, bias)


def _post_kernel(x_ref, oa_ref, ob_ref, gta_ref, gtb_ref, g1_ref, sc2_ref, sh2_ref, g2_ref,
                 nffn_ref, nfin_ref, wa_ref, wb_ref, wo_ref, wfi_ref, wfo_ref, y_ref):
    bb, tl, d = x_ref.shape
    n = bb * tl
    flat = lambda ref: ref[...].reshape(n, ref.shape[-1])
    per_tok = lambda v: v.reshape(bb, tl, d)

    ya = jnp.dot(flat(oa_ref).astype(BF16), wa_ref[...], preferred_element_type=F32)
    yb = jnp.dot(flat(ob_ref).astype(BF16), wb_ref[...], preferred_element_type=F32)
    merged = flat(gta_ref) * ya + flat(gtb_ref) * yb
    y = jnp.dot(merged.astype(BF16), wo_ref[...], preferred_element_type=F32)
    x1 = x_ref[...] + g1_ref[...] * per_tok(y)

    h2 = _rms(x1, nffn_ref[...]) * (1.0 + sc2_ref[...]) + sh2_ref[...]
    hb = h2.reshape(n, d).astype(BF16)
    acc = jnp.zeros((n, d), F32)
    for lo in range(0, D_FF, _FFN_CHUNK):
        a = jnp.dot(hb, wfi_ref[:, lo:lo + _FFN_CHUNK], preferred_element_type=F32)
        u = jnp.dot(hb, wfi_ref[:, D_FF + lo:D_FF + lo + _FFN_CHUNK],
                    preferred_element_type=F32)
        acc = acc + jnp.dot((_silu(a) * u).astype(BF16), wfo_ref[lo:lo + _FFN_CHUNK, :],
                            preferred_element_type=F32)
    x2 = x1 + g2_ref[...] * per_tok(acc)
    y_ref[...] = _rms(x2, nfin_ref[...])


def _post(x, oa, ob, gta, gtb, g1, sc2, sh2, g2, norm_ffn, norm_final,
          wa, wb, wo, wfi, wfo, bb, tl):
    b, l, d = x.shape
    tok = lambda w: pl.BlockSpec((bb, tl, w), lambda i, j: (i, j, 0))
    per_b = pl.BlockSpec((bb, 1, d), lambda i, j: (i, 0, 0))
    return pl.pallas_call(
        _post_kernel,
        out_shape=jax.ShapeDtypeStruct((b, l, d), F32),
        grid=(b // bb, l // tl),
        in_specs=[tok(d), tok(V_A), tok(W_B), tok(d), tok(d), per_b, per_b, per_b, per_b,
                  _resident((1, d)), _resident((1, d)), _resident(wa.shape),
                  _resident(wb.shape), _resident(wo.shape), _resident(wfi.shape),
                  _resident(wfo.shape)],
        out_specs=tok(d),
        compiler_params=_params("arbitrary", "arbitrary"),
        name="merge_out_ffn",
    )(x, oa, ob, gta, gtb, g1, sc2, sh2, g2, norm_ffn, norm_final, wa, wb, wo, wfi, wfo)


def _trunk(x, c, s0, cache_k, cache_v, bias, w, *, dense_tile, hgrn_tile, attn_tile):
    b, l, d = x.shape
    bb, tl = dense_tile
    mod = _modulation(c, w["w_ada"], w["b_ada"])
    sh1, sc1, g1, sh2, sc2, g2 = [m.reshape(b, 1, d) for m in jnp.split(mod, 6, axis=-1)]

    qa, lf, ka, va, gs, qb, kb, vb, gta, gtb = _inproj(
        x, sc1, sh1, w["norm_mix"], w["lb_logits"], w["w_in"], bb, tl)

    oa, s_fin = _hgrn(qa, lf, ka, va, gs, w["out_norm"], s0, hgrn_tile)
    if cache_k is None:
        ob = _attention(qb, kb, vb, None, None, bias, attn_tile, True)
        k_rows, v_rows = kb[:, l - WINDOW:], vb[:, l - WINDOW:]
    else:
        ob = _attention(qb, kb, vb, cache_k, cache_v, bias, attn_tile, False)
        k_rows, v_rows = kb, vb

    y = _post(x, oa, ob, gta, gtb, g1, sc2, sh2, g2, w["norm_ffn"], w["norm_final"],
              w["w_branch_a"], w["w_branch_b"], w["w_out"], w["w_ffn_in"], w["w_ffn_out"],
              bb, tl)
    heads = lambda r: r.reshape(1, b, r.shape[1], H_B, DH_B)
    return y, s_fin[None], heads(k_rows), heads(v_rows)


def kernel(x_prompt, x_sample, c_prompt, c_sample, state_hgrn, cache_k, cache_v, w_ada, b_ada,
           norm_mix, w_in, hgrn_lb_logits, hgrn_out_norm, w_branch_a, rel_bias, w_branch_b,
           w_out, norm_ffn, w_ffn_in, w_ffn_out, norm_final):
    assert w_ada.shape[0] == 1, "single-layer trunk"
    w = dict(
        w_ada=w_ada[0], b_ada=b_ada[0], norm_mix=norm_mix[0].reshape(1, D_MODEL),
        lb_logits=hgrn_lb_logits, w_in=w_in[0].astype(BF16),
        out_norm=hgrn_out_norm[0].reshape(1, V_A),
        w_branch_a=w_branch_a[0].astype(BF16), w_branch_b=w_branch_b[0].astype(BF16),
        w_out=w_out[0].astype(BF16), norm_ffn=norm_ffn[0].reshape(1, D_MODEL),
        w_ffn_in=w_ffn_in[0].astype(BF16), w_ffn_out=w_ffn_out[0].astype(BF16),
        norm_final=norm_final.reshape(1, D_MODEL),
    )
    bias = _bias_tile(rel_bias[0])

    bp = x_prompt.shape[0]
    s0_prompt = jnp.zeros((bp, H_A, DK_A, DV_A), F32)
    y_p, s_p, k_p, v_p = _trunk(x_prompt, c_prompt, s0_prompt, None, None, bias, w,
                                dense_tile=(1, 256), hgrn_tile=256, attn_tile=WINDOW)

    bs, ls = x_sample.shape[:2]
    y_s, s_s, k_s, v_s = _trunk(x_sample, c_sample, state_hgrn[0],
                                cache_k[0].reshape(bs, -1, W_B), cache_v[0].reshape(bs, -1, W_B),
                                bias, w, dense_tile=(4, ls), hgrn_tile=ls, attn_tile=ls)
    return (y_p, y_s, s_p, k_p, v_p, s_s, k_s, v_s)
```

```python
import functools

import jax
import jax.numpy as jnp
from jax import lax
from jax.experimental import pallas as pl
from jax.experimental.pallas import tpu as pltpu

D_MODEL = 1024
CHUNK = 64
H_A, DK_A, DV_A = 4, 128, 128
K_A = H_A * DK_A
V_A = H_A * DV_A
H_B, DH_B = 8, 64
W_B = H_B * DH_B
N_PAST_CHUNKS = 8
WINDOW = N_PAST_CHUNKS * CHUNK
BAND = WINDOW + CHUNK
REL_CLIP = 128
NUM_REL = CHUNK + REL_CLIP
D_FF = 2816
IN_COLS = 4 * K_A + 3 * W_B + 2 * D_MODEL
EPS = 1e-6
NEG = -1e30

F32 = jnp.float32
BF16 = jnp.bfloat16

_OFF_QA, _OFF_FA, _OFF_IA, _OFF_GA = 0, K_A, 2 * K_A, 2 * K_A + V_A
_OFF_QB = 2 * K_A + 2 * V_A
_OFF_KB = _OFF_QB + W_B
_OFF_VB = _OFF_KB + W_B
_OFF_GATE_A = _OFF_VB + W_B
_OFF_GATE_B = _OFF_GATE_A + D_MODEL

_VMEM_LIMIT_BYTES = 56 * 1024 * 1024
_FFN_CHUNK = D_FF // 2


def _params(*sem):
    return pltpu.CompilerParams(dimension_semantics=sem, vmem_limit_bytes=_VMEM_LIMIT_BYTES)


def _resident(shape):
    zeros = (0,) * len(shape)
    return pl.BlockSpec(shape, lambda *_: zeros, pipeline_mode=pl.Buffered(1))


def _silu(x):
    return x * jax.nn.sigmoid(x)


def _rms(x, g):
    return x * lax.rsqrt(jnp.mean(x * x, axis=-1, keepdims=True) + EPS) * g


def _mod_kernel(c_ref, w_ref, b_ref, o_ref):
    s = _silu(c_ref[...])
    o_ref[...] = jnp.dot(s, w_ref[...], precision=lax.Precision.HIGHEST,
                         preferred_element_type=F32) + b_ref[...]


def _modulation(c, w_ada, b_ada):
    b = c.shape[0]
    n_out = w_ada.shape[1]
    return pl.pallas_call(
        _mod_kernel,
        out_shape=jax.ShapeDtypeStruct((b, n_out), F32),
        grid=(n_out // D_MODEL,),
        in_specs=[
            pl.BlockSpec((b, D_MODEL), lambda j: (0, 0)),
            pl.BlockSpec((D_MODEL, D_MODEL), lambda j: (0, j)),
            pl.BlockSpec((1, D_MODEL), lambda j: (0, j)),
        ],
        out_specs=pl.BlockSpec((b, D_MODEL), lambda j: (0, j)),
        compiler_params=_params("arbitrary"),
        name="adaln_mod",
    )(c, w_ada, b_ada.reshape(1, n_out))


def _bias_kernel(rb_ref, o_ref):
    i = pl.program_id(0)
    key = lax.broadcasted_iota(jnp.int32, (NUM_REL, BAND), 1)
    slot = lax.broadcasted_iota(jnp.int32, (NUM_REL, BAND), 0)
    idx = jnp.clip(i - key + WINDOW, -(CHUNK - 1), REL_CLIP) + (CHUNK - 1)
    onehot = jnp.where(slot == idx, 1.0, 0.0).astype(F32)
    o_ref[0] = jnp.dot(rb_ref[...], onehot, precision=lax.Precision.HIGHEST,
                       preferred_element_type=F32)


def _bias_tile(rel_bias):
    out = pl.pallas_call(
        _bias_kernel,
        out_shape=jax.ShapeDtypeStruct((CHUNK, H_B, BAND), F32),
        grid=(CHUNK,),
        in_specs=[pl.BlockSpec((H_B, NUM_REL), lambda i: (0, 0))],
        out_specs=pl.BlockSpec((1, H_B, BAND), lambda i: (i, 0, 0)),
        compiler_params=_params("arbitrary"),
        name="rel_bias_tile",
    )(rel_bias)
    return out.transpose(1, 0, 2)


def _inproj_kernel(x_ref, sc_ref, sh_ref, g_ref, lbl_ref, w_ref,
                   qa_ref, lf_ref, ka_ref, va_ref, gs_ref, qb_ref, kb_ref, vb_ref,
                   gta_ref, gtb_ref):
    bb, tl, d = x_ref.shape
    x = x_ref[...]
    h = _rms(x, g_ref[...]) * (1.0 + sc_ref[...]) + sh_ref[...]
    hb = h.reshape(bb * tl, d).astype(BF16)

    def proj(off, width):
        return jnp.dot(hb, w_ref[:, off:off + width], preferred_element_type=F32)

    def put(ref, val):
        ref[...] = val.reshape(ref.shape)

    lbl = lbl_ref[...]
    e = jnp.exp(lbl - jnp.max(lbl, axis=0, keepdims=True))
    lb = e[0:1] / jnp.sum(e, axis=0, keepdims=True)

    put(qa_ref, _silu(proj(_OFF_QA, K_A)))
    f = lb + (1.0 - lb) * jax.nn.sigmoid(proj(_OFF_FA, K_A))
    put(lf_ref, jnp.log(f))
    put(ka_ref, 1.0 - f)
    put(va_ref, proj(_OFF_IA, V_A))
    put(gs_ref, _silu(proj(_OFF_GA, V_A)))
    put(qb_ref, proj(_OFF_QB, W_B) * (DH_B ** -0.5))
    put(kb_ref, proj(_OFF_KB, W_B))
    put(vb_ref, proj(_OFF_VB, W_B))
    put(gta_ref, jax.nn.sigmoid(proj(_OFF_GATE_A, D_MODEL)))
    put(gtb_ref, jax.nn.sigmoid(proj(_OFF_GATE_B, D_MODEL)))


def _inproj(x, sc, sh, norm_g, lb_logits, w_in_bf16, bb, tl):
    b, l, d = x.shape
    widths = [K_A, K_A, K_A, V_A, V_A, W_B, W_B, W_B, D_MODEL, D_MODEL]
    tok = lambda w: pl.BlockSpec((bb, tl, w), lambda i, j: (i, j, 0))
    per_b = pl.BlockSpec((bb, 1, d), lambda i, j: (i, 0, 0))
    return pl.pallas_call(
        _inproj_kernel,
        out_shape=[jax.ShapeDtypeStruct((b, l, w), F32) for w in widths],
        grid=(b // bb, l // tl),
        in_specs=[tok(d), per_b, per_b, _resident((1, d)), _resident(lb_logits.shape),
                  _resident(w_in_bf16.shape)],
        out_specs=[tok(w) for w in widths],
        compiler_params=_params("arbitrary", "arbitrary"),
        name="norm_inproj",
    )(x, sc, sh, norm_g, lb_logits, w_in_bf16)


def _split3(x):
    hi = x.astype(BF16)
    r = x - hi.astype(F32)
    mid = r.astype(BF16)
    lo = (r - mid.astype(F32)).astype(BF16)
    return hi, mid, lo


def _hgrn_kernel(qa_ref, lf_ref, ka_ref, va_ref, gs_ref, og_ref, s0_ref,
                 o_ref, sfin_ref, st_ref):
    j = pl.program_id(1)
    n_chunks = qa_ref.shape[0] // CHUNK

    @pl.when(j == 0)
    def _():
        for h in range(H_A):
            st_ref[h] = s0_ref[h].T

    row = lax.broadcasted_iota(jnp.int32, (CHUNK, CHUNK), 0)
    col = lax.broadcasted_iota(jnp.int32, (CHUNK, CHUNK), 1)
    tri = row >= col
    tri_bf = jnp.where(tri, 1.0, 0.0).astype(BF16)
    nt = (((1,), (1,)), ((), ()))
    tn = (((0,), (0,)), ((), ()))

    for c in range(n_chunks):
        rows = slice(c * CHUNK, (c + 1) * CHUNK)
        cum = sum(jnp.dot(tri_bf, piece, preferred_element_type=F32)
                  for piece in _split3(lf_ref[rows, :]))
        for h in range(H_A):
            lanes = slice(h * DK_A, (h + 1) * DK_A)
            cum_h = cum[:, lanes]
            tot = cum_h[CHUNK - 1:CHUNK]
            mid = cum_h[CHUNK // 2:CHUNK // 2 + 1]
            q = qa_ref[rows, lanes]
            k = ka_ref[rows, lanes]
            v = va_ref[rows, lanes].astype(BF16)
            st = st_ref[h]
            a = lax.dot_general((q * jnp.exp(cum_h - mid)).astype(BF16),
                                (k * jnp.exp(mid - cum_h)).astype(BF16), nt,
                                preferred_element_type=F32)
            a = jnp.where(tri, a, 0.0).astype(BF16)
            o = (jnp.dot(a, v, preferred_element_type=F32)
                 + lax.dot_general((q * jnp.exp(cum_h)).astype(BF16), st.astype(BF16), nt,
                                   preferred_element_type=F32))
            st_ref[h] = (jnp.exp(tot) * st
                         + lax.dot_general(v, (k * jnp.exp(tot - cum_h)).astype(BF16), tn,
                                           preferred_element_type=F32))
            o = o * lax.rsqrt(jnp.mean(o * o, axis=-1, keepdims=True) + EPS)
            o_ref[rows, lanes] = o * og_ref[:, lanes] * gs_ref[rows, lanes]

    @pl.when(j == pl.num_programs(1) - 1)
    def _():
        for h in range(H_A):
            sfin_ref[h] = st_ref[h].T


def _hgrn(qa, lf, ka, va, gs, out_norm, s0, th):
    b, l, _ = qa.shape
    tok = pl.BlockSpec((None, th, K_A), lambda i, j: (i, j, 0))
    state = pl.BlockSpec((None, H_A, DK_A, DV_A), lambda i, j: (i, 0, 0, 0))
    return pl.pallas_call(
        _hgrn_kernel,
        out_shape=[jax.ShapeDtypeStruct((b, l, V_A), F32),
                   jax.ShapeDtypeStruct((b, H_A, DK_A, DV_A), F32)],
        grid=(b, l // th),
        in_specs=[tok, tok, tok, tok, tok, _resident((1, V_A)), state],
        out_specs=[tok, state],
        scratch_shapes=[pltpu.VMEM((H_A, DV_A, DK_A), F32)],
        compiler_params=_params("arbitrary", "arbitrary"),
        name="hgrn2_recurrence",
    )(qa, lf, ka, va, gs, out_norm, s0)


_HEADS_PER_UNIT = 4
_UNIT_LANES = _HEADS_PER_UNIT * DH_B
_UNIT_ROWS = _HEADS_PER_UNIT * CHUNK


def _attn_kernel(q_ref, kp_ref, kc_ref, vp_ref, vc_ref, bias_ref, o_ref, kbuf, vbuf, *,
                 mask_first_tile):
    t = pl.program_id(1)
    bb, tq, _ = q_ref.shape
    kbuf[:, 0:WINDOW, :] = kp_ref[...].astype(BF16)
    kbuf[:, WINDOW:WINDOW + tq, :] = kc_ref[...].astype(BF16)
    vbuf[:, 0:WINDOW, :] = vp_ref[...].astype(BF16)
    vbuf[:, WINDOW:WINDOW + tq, :] = vc_ref[...].astype(BF16)

    lane_head = lax.broadcasted_iota(jnp.int32, (CHUNK, _UNIT_LANES), 1) // DH_B
    own = [lane_head == h for h in range(_HEADS_PER_UNIT)]
    key = lax.broadcasted_iota(jnp.int32, (_UNIT_ROWS, BAND), 1)
    nt = (((1,), (1,)), ((), ()))

    for bi in range(bb):
        for c in range(tq // CHUNK):
            rows = slice(c * CHUNK, (c + 1) * CHUNK)
            band = slice(c * CHUNK, c * CHUNK + BAND)
            valid = (key + c * CHUNK >= WINDOW) | (t > 0) if mask_first_tile else None
            for g in range(H_B // _HEADS_PER_UNIT):
                lanes = slice(g * _UNIT_LANES, (g + 1) * _UNIT_LANES)
                q4 = q_ref[bi, rows, lanes]
                qs = jnp.concatenate([jnp.where(m, q4, 0.0) for m in own], axis=0).astype(BF16)
                s = lax.dot_general(qs, kbuf[bi, band, lanes], nt, preferred_element_type=F32)
                s = s + bias_ref[g]
                if valid is not None:
                    s = jnp.where(valid, s, NEG)
                p = jnp.exp(s - jnp.max(s, axis=-1, keepdims=True))
                inv = 1.0 / jnp.sum(p, axis=-1, keepdims=True)
                o4 = jnp.dot(p.astype(BF16), vbuf[bi, band, lanes],
                             preferred_element_type=F32) * inv
                o = jnp.where(own[0], o4[0:CHUNK], 0.0)
                for h in range(1, _HEADS_PER_UNIT):
                    o = jnp.where(own[h], o4[h * CHUNK:(h + 1) * CHUNK], o)
                o_ref[bi, rows, lanes] = o


def _attention(q, k, v, k_prev, v_prev, bias, bb, tq, mask_first_tile):
    b, l, _ = q.shape
    cur = pl.BlockSpec((bb, tq, W_B), lambda i, j: (i, j, 0))
    if k_prev is None:
        assert tq == WINDOW
        k_prev, v_prev = k, v
        prev = pl.BlockSpec((bb, WINDOW, W_B), lambda i, j: (i, jnp.maximum(j - 1, 0), 0))
    else:
        prev = pl.BlockSpec((bb, WINDOW, W_B), lambda i, j: (i, 0, 0))
    bias = bias.reshape(H_B // _HEADS_PER_UNIT, _UNIT_ROWS, BAND)
    return pl.pallas_call(
        functools.partial(_attn_kernel, mask_first_tile=mask_first_tile),
        out_shape=jax.ShapeDtypeStruct((b, l, W_B), F32),
        grid=(b // bb, l // tq),
        in_specs=[cur, prev, cur, prev, cur, _resident(bias.shape)],
        out_specs=cur,
        scratch_shapes=[pltpu.VMEM((bb, WINDOW + tq, W_B), BF16),
                        pltpu.VMEM((bb, WINDOW + tq, W_B), BF16)],
        compiler_params=_params("arbitrary", "arbitrary"),
        name="band_attention",
    )(q, k_prev, k, v_prev, v, bias)


def _post_kernel(x_ref, oa_ref, ob_ref, gta_ref, gtb_ref, g1_ref, sc2_ref, sh2_ref, g2_ref,
                 nffn_ref, nfin_ref, wa_ref, wb_ref, wo_ref, wfi_ref, wfo_ref, y_ref):
    bb, tl, d = x_ref.shape
    n = bb * tl
    flat = lambda ref: ref[...].reshape(n, ref.shape[-1])
    per_tok = lambda v: v.reshape(bb, tl, d)

    ya = jnp.dot(flat(oa_ref).astype(BF16), wa_ref[...], preferred_element_type=F32)
    yb = jnp.dot(flat(ob_ref).astype(BF16), wb_ref[...], preferred_element_type=F32)
    merged = flat(gta_ref) * ya + flat(gtb_ref) * yb
    y = jnp.dot(merged.astype(BF16), wo_ref[...], preferred_element_type=F32)
    x1 = x_ref[...] + g1_ref[...] * per_tok(y)

    h2 = _rms(x1, nffn_ref[...]) * (1.0 + sc2_ref[...]) + sh2_ref[...]
    hb = h2.reshape(n, d).astype(BF16)
    acc = jnp.zeros((n, d), F32)
    for lo in range(0, D_FF, _FFN_CHUNK):
        a = jnp.dot(hb, wfi_ref[:, lo:lo + _FFN_CHUNK], preferred_element_type=F32)
        u = jnp.dot(hb, wfi_ref[:, D_FF + lo:D_FF + lo + _FFN_CHUNK],
                    preferred_element_type=F32)
        acc = acc + jnp.dot((_silu(a) * u).astype(BF16), wfo_ref[lo:lo + _FFN_CHUNK, :],
                            preferred_element_type=F32)
    x2 = x1 + g2_ref[...] * per_tok(acc)
    y_ref[...] = _rms(x2, nfin_ref[...])


def _post(x, oa, ob, gta, gtb, g1, sc2, sh2, g2, norm_ffn, norm_final,
          wa, wb, wo, wfi, wfo, bb, tl):
    b, l, d = x.shape
    tok = lambda w: pl.BlockSpec((bb, tl, w), lambda i, j: (i, j, 0))
    per_b = pl.BlockSpec((bb, 1, d), lambda i, j: (i, 0, 0))
    return pl.pallas_call(
        _post_kernel,
        out_shape=jax.ShapeDtypeStruct((b, l, d), F32),
        grid=(b // bb, l // tl),
        in_specs=[tok(d), tok(V_A), tok(W_B), tok(d), tok(d), per_b, per_b, per_b, per_b,
                  _resident((1, d)), _resident((1, d)), _resident(wa.shape),
                  _resident(wb.shape), _resident(wo.shape), _resident(wfi.shape),
                  _resident(wfo.shape)],
        out_specs=tok(d),
        compiler_params=_params("arbitrary", "arbitrary"),
        name="merge_out_ffn",
    )(x, oa, ob, gta, gtb, g1, sc2, sh2, g2, norm_ffn, norm_final, wa, wb, wo, wfi, wfo)


def _trunk(x, c, s0, cache_k, cache_v, bias, w, *, dense_tile, hgrn_tile, attn_tile):
    b, l, d = x.shape
    bb, tl = dense_tile
    mod = _modulation(c, w["w_ada"], w["b_ada"])
    sh1, sc1, g1, sh2, sc2, g2 = [m.reshape(b, 1, d) for m in jnp.split(mod, 6, axis=-1)]

    qa, lf, ka, va, gs, qb, kb, vb, gta, gtb = _inproj(
        x, sc1, sh1, w["norm_mix"], w["lb_logits"], w["w_in"], bb, tl)

    oa, s_fin = _hgrn(qa, lf, ka, va, gs, w["out_norm"], s0, hgrn_tile)
    abb, atq = attn_tile
    if cache_k is None:
        ob = _attention(qb, kb, vb, None, None, bias, abb, atq, True)
        k_rows, v_rows = kb[:, l - WINDOW:], vb[:, l - WINDOW:]
    else:
        ob = _attention(qb, kb, vb, cache_k, cache_v, bias, abb, atq, False)
        k_rows, v_rows = kb, vb

    y = _post(x, oa, ob, gta, gtb, g1, sc2, sh2, g2, w["norm_ffn"], w["norm_final"],
              w["w_branch_a"], w["w_branch_b"], w["w_out"], w["w_ffn_in"], w["w_ffn_out"],
              bb, tl)
    heads = lambda r: r.reshape(1, b, r.shape[1], H_B, DH_B)
    return y, s_fin[None], heads(k_rows), heads(v_rows)


def kernel(x_prompt, x_sample, c_prompt, c_sample, state_hgrn, cache_k, cache_v, w_ada, b_ada,
           norm_mix, w_in, hgrn_lb_logits, hgrn_out_norm, w_branch_a, rel_bias, w_branch_b,
           w_out, norm_ffn, w_ffn_in, w_ffn_out, norm_final):
    assert w_ada.shape[0] == 1, "single-layer trunk"
    w = dict(
        w_ada=w_ada[0], b_ada=b_ada[0], norm_mix=norm_mix[0].reshape(1, D_MODEL),
        lb_logits=hgrn_lb_logits, w_in=w_in[0].astype(BF16),
        out_norm=hgrn_out_norm[0].reshape(1, V_A),
        w_branch_a=w_branch_a[0].astype(BF16), w_branch_b=w_branch_b[0].astype(BF16),
        w_out=w_out[0].astype(BF16), norm_ffn=norm_ffn[0].reshape(1, D_MODEL),
        w_ffn_in=w_ffn_in[0].astype(BF16), w_ffn_out=w_ffn_out[0].astype(BF16),
        norm_final=norm_final.reshape(1, D_MODEL),
    )
    bias = _bias_tile(rel_bias[0])

    bp = x_prompt.shape[0]
    s0_prompt = jnp.zeros((bp, H_A, DK_A, DV_A), F32)
    y_p, s_p, k_p, v_p = _trunk(x_prompt, c_prompt, s0_prompt, None, None, bias, w,
                                dense_tile=(1, 256), hgrn_tile=256, attn_tile=(1, WINDOW))

    bs, ls = x_sample.shape[:2]
    y_s, s_s, k_s, v_s = _trunk(x_sample, c_sample, state_hgrn[0],
                                cache_k[0].reshape(bs, -1, W_B), cache_v[0].reshape(bs, -1, W_B),
                                bias, w, dense_tile=(4, ls), hgrn_tile=ls, attn_tile=(4, ls))
    return (y_p, y_s, s_p, k_p, v_p, s_s, k_s, v_s)
```

```python
import functools

import jax
import jax.numpy as jnp
from jax import lax
from jax.experimental import pallas as pl
from jax.experimental.pallas import tpu as pltpu

D_MODEL = 1024
CHUNK = 64
H_A, DK_A, DV_A = 4, 128, 128
K_A = H_A * DK_A
V_A = H_A * DV_A
H_B, DH_B = 8, 64
W_B = H_B * DH_B
N_PAST_CHUNKS = 8
WINDOW = N_PAST_CHUNKS * CHUNK
BAND = WINDOW + CHUNK
REL_CLIP = 128
NUM_REL = CHUNK + REL_CLIP
D_FF = 2816
IN_COLS = 4 * K_A + 3 * W_B + 2 * D_MODEL
EPS = 1e-6
NEG = -1e30

F32 = jnp.float32
BF16 = jnp.bfloat16

_OFF_QA, _OFF_FA, _OFF_IA, _OFF_GA = 0, K_A, 2 * K_A, 2 * K_A + V_A
_OFF_QB = 2 * K_A + 2 * V_A
_OFF_KB = _OFF_QB + W_B
_OFF_VB = _OFF_KB + W_B
_OFF_GATE_A = _OFF_VB + W_B
_OFF_GATE_B = _OFF_GATE_A + D_MODEL

_VMEM_LIMIT_BYTES = 56 * 1024 * 1024
_FFN_CHUNK = D_FF // 2


def _params(*sem):
    return pltpu.CompilerParams(dimension_semantics=sem, vmem_limit_bytes=_VMEM_LIMIT_BYTES)


def _resident(shape):
    zeros = (0,) * len(shape)
    return pl.BlockSpec(shape, lambda *_: zeros, pipeline_mode=pl.Buffered(1))


def _silu(x):
    return x * jax.nn.sigmoid(x)


def _rms(x, g):
    return x * lax.rsqrt(jnp.mean(x * x, axis=-1, keepdims=True) + EPS) * g


def _split3(x):
    hi = x.astype(BF16)
    r = x - hi.astype(F32)
    mid = r.astype(BF16)
    lo = (r - mid.astype(F32)).astype(BF16)
    return hi, mid, lo


def _mod_kernel(c_ref, w_ref, b_ref, o_ref):
    s_hi, s_lo, _ = _split3(_silu(c_ref[...]))
    w_hi, w_lo, _ = _split3(w_ref[...])
    dot = lambda a, b: jnp.dot(a, b, preferred_element_type=F32)
    o_ref[...] = dot(s_hi, w_hi) + (dot(s_lo, w_hi) + dot(s_hi, w_lo)) + b_ref[...]


def _modulation(c, w_ada, b_ada):
    b = c.shape[0]
    n_out = w_ada.shape[1]
    return pl.pallas_call(
        _mod_kernel,
        out_shape=jax.ShapeDtypeStruct((b, n_out), F32),
        grid=(n_out // D_MODEL,),
        in_specs=[
            pl.BlockSpec((b, D_MODEL), lambda j: (0, 0)),
            pl.BlockSpec((D_MODEL, D_MODEL), lambda j: (0, j)),
            pl.BlockSpec((1, D_MODEL), lambda j: (0, j)),
        ],
        out_specs=pl.BlockSpec((b, D_MODEL), lambda j: (0, j)),
        compiler_params=_params("arbitrary"),
        name="adaln_mod",
    )(c, w_ada, b_ada.reshape(1, n_out))


_BIAS_EXT = 640


def _bias_kernel(rb_ref, o_ref):
    m = lax.broadcasted_iota(jnp.int32, (NUM_REL, _BIAS_EXT), 1)
    slot = lax.broadcasted_iota(jnp.int32, (NUM_REL, _BIAS_EXT), 0)
    idx = jnp.clip(WINDOW + CHUNK - 1 - m, -(CHUNK - 1), REL_CLIP) + (CHUNK - 1)
    onehot = jnp.where(slot == idx, 1.0, 0.0).astype(BF16)
    ext = sum(jnp.dot(piece, onehot, preferred_element_type=F32)
              for piece in _split3(rb_ref[...]))
    for i in range(CHUNK):
        o_ref[i] = ext[:, CHUNK - 1 - i:CHUNK - 1 - i + BAND]


def _bias_tile(rel_bias):
    out = pl.pallas_call(
        _bias_kernel,
        out_shape=jax.ShapeDtypeStruct((CHUNK, H_B, BAND), F32),
        grid=(1,),
        in_specs=[pl.BlockSpec((H_B, NUM_REL), lambda i: (0, 0))],
        out_specs=pl.BlockSpec((CHUNK, H_B, BAND), lambda i: (0, 0, 0)),
        compiler_params=_params("arbitrary"),
        name="rel_bias_tile",
    )(rel_bias)
    return out.transpose(1, 0, 2)


def _inproj_kernel(x_ref, sc_ref, sh_ref, g_ref, lbl_ref, w_ref,
                   qa_ref, lf_ref, ka_ref, va_ref, gs_ref, qb_ref, kb_ref, vb_ref,
                   gta_ref, gtb_ref):
    bb, tl, d = x_ref.shape
    x = x_ref[...]
    h = _rms(x, g_ref[...]) * (1.0 + sc_ref[...]) + sh_ref[...]
    hb = h.reshape(bb * tl, d).astype(BF16)

    def proj(off, width):
        return jnp.dot(hb, w_ref[:, off:off + width], preferred_element_type=F32)

    def put(ref, val):
        ref[...] = val.reshape(ref.shape)

    lbl = lbl_ref[...]
    e = jnp.exp(lbl - jnp.max(lbl, axis=0, keepdims=True))
    lb = e[0:1] / jnp.sum(e, axis=0, keepdims=True)

    put(qa_ref, _silu(proj(_OFF_QA, K_A)))
    f = lb + (1.0 - lb) * jax.nn.sigmoid(proj(_OFF_FA, K_A))
    put(lf_ref, jnp.log(f))
    put(ka_ref, 1.0 - f)
    put(va_ref, proj(_OFF_IA, V_A))
    put(gs_ref, _silu(proj(_OFF_GA, V_A)))
    put(qb_ref, proj(_OFF_QB, W_B) * (DH_B ** -0.5))
    put(kb_ref, proj(_OFF_KB, W_B))
    put(vb_ref, proj(_OFF_VB, W_B))
    put(gta_ref, jax.nn.sigmoid(proj(_OFF_GATE_A, D_MODEL)))
    put(gtb_ref, jax.nn.sigmoid(proj(_OFF_GATE_B, D_MODEL)))


def _inproj(x, sc, sh, norm_g, lb_logits, w_in_bf16, bb, tl):
    b, l, d = x.shape
    widths = [K_A, K_A, K_A, V_A, V_A, W_B, W_B, W_B, D_MODEL, D_MODEL]
    tok = lambda w: pl.BlockSpec((bb, tl, w), lambda i, j: (i, j, 0))
    per_b = pl.BlockSpec((bb, 1, d), lambda i, j: (i, 0, 0))
    return pl.pallas_call(
        _inproj_kernel,
        out_shape=[jax.ShapeDtypeStruct((b, l, w), F32) for w in widths],
        grid=(b // bb, l // tl),
        in_specs=[tok(d), per_b, per_b, _resident((1, d)), _resident(lb_logits.shape),
                  _resident(w_in_bf16.shape)],
        out_specs=[tok(w) for w in widths],
        compiler_params=_params("arbitrary", "arbitrary"),
        name="norm_inproj",
    )(x, sc, sh, norm_g, lb_logits, w_in_bf16)


def _hgrn_kernel(qa_ref, lf_ref, ka_ref, va_ref, gs_ref, og_ref, s0_ref,
                 o_ref, sfin_ref, st_ref):
    j = pl.program_id(1)
    bb, th, _ = qa_ref.shape
    nc = th // CHUNK

    @pl.when(j == 0)
    def _():
        for bi in range(bb):
            for h in range(H_A):
                st_ref[bi, h] = s0_ref[bi, h].T

    row = lax.broadcasted_iota(jnp.int32, (th, th), 0)
    col = lax.broadcasted_iota(jnp.int32, (th, th), 1)
    tri = (row >= col) & (row // CHUNK == col // CHUNK)
    tri_bf = jnp.where(tri, 1.0, 0.0).astype(BF16)
    nt = (((1,), (1,)), ((), ()))
    tn = (((0,), (0,)), ((), ()))
    chunked = lambda a: a.reshape(nc, CHUNK, a.shape[-1])
    flat = lambda a: a.reshape(th, a.shape[-1])

    for bi in range(bb):
        cum = chunked(sum(jnp.dot(tri_bf, piece, preferred_element_type=F32)
                          for piece in _split3(lf_ref[bi])))
        mid = cum[:, CHUNK // 2:CHUNK // 2 + 1]
        tot = cum[:, CHUNK - 1:CHUNK]
        q = chunked(qa_ref[bi])
        k = chunked(ka_ref[bi])
        v = va_ref[bi].astype(BF16)
        qe = flat(q * jnp.exp(cum - mid)).astype(BF16)
        ke = flat(k * jnp.exp(mid - cum)).astype(BF16)
        qd = flat(q * jnp.exp(cum)).astype(BF16)
        kd = flat(k * jnp.exp(tot - cum)).astype(BF16)
        decay = jnp.exp(tot)
        for h in range(H_A):
            lanes = slice(h * DK_A, (h + 1) * DK_A)
            a = lax.dot_general(qe[:, lanes], ke[:, lanes], nt, preferred_element_type=F32)
            a = jnp.where(tri, a, 0.0).astype(BF16)
            intra = jnp.dot(a, v[:, lanes], preferred_element_type=F32)
            st = st_ref[bi, h]
            inter = []
            for c in range(nc):
                rows = slice(c * CHUNK, (c + 1) * CHUNK)
                inter.append(lax.dot_general(qd[rows, lanes], st.astype(BF16), nt,
                                             preferred_element_type=F32))
                st = decay[c, :, lanes] * st + lax.dot_general(
                    v[rows, lanes], kd[rows, lanes], tn, preferred_element_type=F32)
            st_ref[bi, h] = st
            o = intra + jnp.concatenate(inter, axis=0)
            o = o * lax.rsqrt(jnp.mean(o * o, axis=-1, keepdims=True) + EPS)
            o_ref[bi, :, lanes] = o * og_ref[:, lanes] * gs_ref[bi, :, lanes]

    @pl.when(j == pl.num_programs(1) - 1)
    def _():
        for bi in range(bb):
            for h in range(H_A):
                sfin_ref[bi, h] = st_ref[bi, h].T


def _hgrn(qa, lf, ka, va, gs, out_norm, s0, bb, th):
    b, l, _ = qa.shape
    tok = pl.BlockSpec((bb, th, K_A), lambda i, j: (i, j, 0))
    state = pl.BlockSpec((bb, H_A, DK_A, DV_A), lambda i, j: (i, 0, 0, 0))
    return pl.pallas_call(
        _hgrn_kernel,
        out_shape=[jax.ShapeDtypeStruct((b, l, V_A), F32),
                   jax.ShapeDtypeStruct((b, H_A, DK_A, DV_A), F32)],
        grid=(b // bb, l // th),
        in_specs=[tok, tok, tok, tok, tok, _resident((1, V_A)), state],
        out_specs=[tok, state],
        scratch_shapes=[pltpu.VMEM((bb, H_A, DV_A, DK_A), F32)],
        compiler_params=_params("arbitrary", "arbitrary"),
        name="hgrn2_recurrence",
    )(qa, lf, ka, va, gs, out_norm, s0)


_HEADS_PER_UNIT = 4
_UNIT_LANES = _HEADS_PER_UNIT * DH_B
_UNIT_ROWS = _HEADS_PER_UNIT * CHUNK


def _attn_kernel(q_ref, kp_ref, kc_ref, vp_ref, vc_ref, bias_ref, o_ref, kbuf, vbuf, *,
                 mask_first_tile):
    t = pl.program_id(1)
    bb, tq, _ = q_ref.shape
    kbuf[:, 0:WINDOW, :] = kp_ref[...].astype(BF16)
    kbuf[:, WINDOW:WINDOW + tq, :] = kc_ref[...].astype(BF16)
    vbuf[:, 0:WINDOW, :] = vp_ref[...].astype(BF16)
    vbuf[:, WINDOW:WINDOW + tq, :] = vc_ref[...].astype(BF16)

    lane_head = lax.broadcasted_iota(jnp.int32, (CHUNK, _UNIT_LANES), 1) // DH_B
    own = [lane_head == h for h in range(_HEADS_PER_UNIT)]
    key = lax.broadcasted_iota(jnp.int32, (_UNIT_ROWS, BAND), 1)
    nt = (((1,), (1,)), ((), ()))

    for bi in range(bb):
        for c in range(tq // CHUNK):
            rows = slice(c * CHUNK, (c + 1) * CHUNK)
            band = slice(c * CHUNK, c * CHUNK + BAND)
            valid = (key + c * CHUNK >= WINDOW) | (t > 0) if mask_first_tile else None
            for g in range(H_B // _HEADS_PER_UNIT):
                lanes = slice(g * _UNIT_LANES, (g + 1) * _UNIT_LANES)
                q4 = q_ref[bi, rows, lanes]
                qs = jnp.concatenate([jnp.where(m, q4, 0.0) for m in own], axis=0).astype(BF16)
                s = lax.dot_general(qs, kbuf[bi, band, lanes], nt, preferred_element_type=F32)
                s = s + bias_ref[g]
                if valid is not None:
                    s = jnp.where(valid, s, NEG)
                p = jnp.exp(s - jnp.max(s, axis=-1, keepdims=True))
                inv = 1.0 / jnp.sum(p, axis=-1, keepdims=True)
                o4 = jnp.dot(p.astype(BF16), vbuf[bi, band, lanes],
                             preferred_element_type=F32) * inv
                o = jnp.where(own[0], o4[0:CHUNK], 0.0)
                for h in range(1, _HEADS_PER_UNIT):
                    o = jnp.where(own[h], o4[h * CHUNK:(h + 1) * CHUNK], o)
                o_ref[bi, rows, lanes] = o


def _attention(q, k, v, k_prev, v_prev, bias, bb, tq, mask_first_tile):
    b, l, _ = q.shape
    cur = pl.BlockSpec((bb, tq, W_B), lambda i, j: (i, j, 0))
    if k_prev is None:
        assert tq == WINDOW
        k_prev, v_prev = k, v
        prev = pl.BlockSpec((bb, WINDOW, W_B), lambda i, j: (i, jnp.maximum(j - 1, 0), 0))
    else:
        prev = pl.BlockSpec((bb, WINDOW, W_B), lambda i, j: (i, 0, 0))
    bias = bias.reshape(H_B // _HEADS_PER_UNIT, _UNIT_ROWS, BAND)
    return pl.pallas_call(
        functools.partial(_attn_kernel, mask_first_tile=mask_first_tile),
        out_shape=jax.ShapeDtypeStruct((b, l, W_B), F32),
        grid=(b // bb, l // tq),
        in_specs=[cur, prev, cur, prev, cur, _resident(bias.shape)],
        out_specs=cur,
        scratch_shapes=[pltpu.VMEM((bb, WINDOW + tq, W_B), BF16),
                        pltpu.VMEM((bb, WINDOW + tq, W_B), BF16)],
        compiler_params=_params("arbitrary", "arbitrary"),
        name="band_attention",
    )(q, k_prev, k, v_prev, v, bias)


def _post_kernel(x_ref, oa_ref, ob_ref, gta_ref, gtb_ref, g1_ref, sc2_ref, sh2_ref, g2_ref,
                 nffn_ref, nfin_ref, wa_ref, wb_ref, wo_ref, wfi_ref, wfo_ref, y_ref):
    bb, tl, d = x_ref.shape
    n = bb * tl
    flat = lambda ref: ref[...].reshape(n, ref.shape[-1])
    per_tok = lambda v: v.reshape(bb, tl, d)

    ya = jnp.dot(flat(oa_ref).astype(BF16), wa_ref[...], preferred_element_type=F32)
    yb = jnp.dot(flat(ob_ref).astype(BF16), wb_ref[...], preferred_element_type=F32)
    merged = flat(gta_ref) * ya + flat(gtb_ref) * yb
    y = jnp.dot(merged.astype(BF16), wo_ref[...], preferred_element_type=F32)
    x1 = x_ref[...] + g1_ref[...] * per_tok(y)

    h2 = _rms(x1, nffn_ref[...]) * (1.0 + sc2_ref[...]) + sh2_ref[...]
    hb = h2.reshape(n, d).astype(BF16)
    acc = jnp.zeros((n, d), F32)
    for lo in range(0, D_FF, _FFN_CHUNK):
        a = jnp.dot(hb, wfi_ref[:, lo:lo + _FFN_CHUNK], preferred_element_type=F32)
        u = jnp.dot(hb, wfi_ref[:, D_FF + lo:D_FF + lo + _FFN_CHUNK],
                    preferred_element_type=F32)
        acc = acc + jnp.dot((_silu(a) * u).astype(BF16), wfo_ref[lo:lo + _FFN_CHUNK, :],
                            preferred_element_type=F32)
    x2 = x1 + g2_ref[...] * per_tok(acc)
    y_ref[...] = _rms(x2, nfin_ref[...])


def _post(x, oa, ob, gta, gtb, g1, sc2, sh2, g2, norm_ffn, norm_final,
          wa, wb, wo, wfi, wfo, bb, tl):
    b, l, d = x.shape
    tok = lambda w: pl.BlockSpec((bb, tl, w), lambda i, j: (i, j, 0))
    per_b = pl.BlockSpec((bb, 1, d), lambda i, j: (i, 0, 0))
    return pl.pallas_call(
        _post_kernel,
        out_shape=jax.ShapeDtypeStruct((b, l, d), F32),
        grid=(b // bb, l // tl),
        in_specs=[tok(d), tok(V_A), tok(W_B), tok(d), tok(d), per_b, per_b, per_b, per_b,
                  _resident((1, d)), _resident((1, d)), _resident(wa.shape),
                  _resident(wb.shape), _resident(wo.shape), _resident(wfi.shape),
                  _resident(wfo.shape)],
        out_specs=tok(d),
        compiler_params=_params("arbitrary", "arbitrary"),
        name="merge_out_ffn",
    )(x, oa, ob, gta, gtb, g1, sc2, sh2, g2, norm_ffn, norm_final, wa, wb, wo, wfi, wfo)


def _trunk(x, mod, s0, cache_k, cache_v, bias, w, *, in_tile, post_tile, hgrn_tile, attn_tile):
    b, l, d = x.shape
    sh1, sc1, g1, sh2, sc2, g2 = [m.reshape(b, 1, d) for m in jnp.split(mod, 6, axis=-1)]

    qa, lf, ka, va, gs, qb, kb, vb, gta, gtb = _inproj(
        x, sc1, sh1, w["norm_mix"], w["lb_logits"], w["w_in"], *in_tile)

    oa, s_fin = _hgrn(qa, lf, ka, va, gs, w["out_norm"], s0, *hgrn_tile)
    abb, atq = attn_tile
    if cache_k is None:
        ob = _attention(qb, kb, vb, None, None, bias, abb, atq, True)
        k_rows, v_rows = kb[:, l - WINDOW:], vb[:, l - WINDOW:]
    else:
        ob = _attention(qb, kb, vb, cache_k, cache_v, bias, abb, atq, False)
        k_rows, v_rows = kb, vb

    y = _post(x, oa, ob, gta, gtb, g1, sc2, sh2, g2, w["norm_ffn"], w["norm_final"],
              w["w_branch_a"], w["w_branch_b"], w["w_out"], w["w_ffn_in"], w["w_ffn_out"],
              *post_tile)
    heads = lambda r: r.reshape(1, b, r.shape[1], H_B, DH_B)
    return y, s_fin[None], heads(k_rows), heads(v_rows)


def kernel(x_prompt, x_sample, c_prompt, c_sample, state_hgrn, cache_k, cache_v, w_ada, b_ada,
           norm_mix, w_in, hgrn_lb_logits, hgrn_out_norm, w_branch_a, rel_bias, w_branch_b,
           w_out, norm_ffn, w_ffn_in, w_ffn_out, norm_final):
    assert w_ada.shape[0] == 1, "single-layer trunk"
    w = dict(
        w_ada=w_ada[0], b_ada=b_ada[0], norm_mix=norm_mix[0].reshape(1, D_MODEL),
        lb_logits=hgrn_lb_logits, w_in=w_in[0].astype(BF16),
        out_norm=hgrn_out_norm[0].reshape(1, V_A),
        w_branch_a=w_branch_a[0].astype(BF16), w_branch_b=w_branch_b[0].astype(BF16),
        w_out=w_out[0].astype(BF16), norm_ffn=norm_ffn[0].reshape(1, D_MODEL),
        w_ffn_in=w_ffn_in[0].astype(BF16), w_ffn_out=w_ffn_out[0].astype(BF16),
        norm_final=norm_final.reshape(1, D_MODEL),
    )
    bias = _bias_tile(rel_bias[0])

    bp = x_prompt.shape[0]
    bs, ls = x_sample.shape[:2]
    mod = _modulation(jnp.concatenate([c_prompt, c_sample], axis=0), w["w_ada"], w["b_ada"])

    s0_prompt = jnp.zeros((bp, H_A, DK_A, DV_A), F32)
    y_p, s_p, k_p, v_p = _trunk(x_prompt, mod[:bp], s0_prompt, None, None, bias, w,
                                in_tile=(1, 512), post_tile=(1, 256), hgrn_tile=(bp, 256),
                                attn_tile=(1, WINDOW))
    y_s, s_s, k_s, v_s = _trunk(x_sample, mod[bp:], state_hgrn[0],
                                cache_k[0].reshape(bs, -1, W_B), cache_v[0].reshape(bs, -1, W_B),
                                bias, w, in_tile=(8, ls), post_tile=(4, ls), hgrn_tile=(8, ls),
                                attn_tile=(4, ls))
    return (y_p, y_s, s_p, k_p, v_p, s_s, k_s, v_s)
```

```python
import functools

import jax
import jax.numpy as jnp
from jax import lax
from jax.experimental import pallas as pl
from jax.experimental.pallas import tpu as pltpu

D_MODEL = 1024
CHUNK = 64
H_A, DK_A, DV_A = 4, 128, 128
K_A = H_A * DK_A
V_A = H_A * DV_A
H_B, DH_B = 8, 64
W_B = H_B * DH_B
N_PAST_CHUNKS = 8
WINDOW = N_PAST_CHUNKS * CHUNK
BAND = WINDOW + CHUNK
REL_CLIP = 128
NUM_REL = CHUNK + REL_CLIP
D_FF = 2816
IN_COLS = 4 * K_A + 3 * W_B + 2 * D_MODEL
EPS = 1e-6
NEG = -1e30
_LOG2E = 1.4426950408889634
_Q_SCALE = (DH_B ** -0.5) * _LOG2E

F32 = jnp.float32
BF16 = jnp.bfloat16

_OFF_QA, _OFF_FA, _OFF_IA, _OFF_GA = 0, K_A, 2 * K_A, 2 * K_A + V_A
_OFF_QB = 2 * K_A + 2 * V_A
_OFF_KB = _OFF_QB + W_B
_OFF_VB = _OFF_KB + W_B
_OFF_GATE_A = _OFF_VB + W_B
_OFF_GATE_B = _OFF_GATE_A + D_MODEL

_VMEM_LIMIT_BYTES = 56 * 1024 * 1024
_MXU_DIM = 256
_FFN_CHUNKS = ((0, 6 * _MXU_DIM), (6 * _MXU_DIM, D_FF))
assert D_FF % _MXU_DIM == 0


def _params(*sem):
    return pltpu.CompilerParams(dimension_semantics=sem, vmem_limit_bytes=_VMEM_LIMIT_BYTES)


def _resident(shape):
    zeros = (0,) * len(shape)
    return pl.BlockSpec(shape, lambda *_: zeros, pipeline_mode=pl.Buffered(1))


def _sigmoid(x):
    return 0.5 * jnp.tanh(0.5 * x) + 0.5


def _silu(x):
    return x * _sigmoid(x)


def _rms(x, g):
    return x * lax.rsqrt(jnp.mean(x * x, axis=-1, keepdims=True) + EPS) * g


def _split3(x):
    hi = x.astype(BF16)
    r = x - hi.astype(F32)
    mid = r.astype(BF16)
    lo = (r - mid.astype(F32)).astype(BF16)
    return hi, mid, lo


def _mod_kernel(c_ref, w_ref, b_ref, o_ref):
    s_hi, s_lo, _ = _split3(_silu(c_ref[...]))
    w_hi, w_lo, _ = _split3(w_ref[...])
    dot = lambda a, b: jnp.dot(a, b, preferred_element_type=F32)
    o_ref[...] = dot(s_hi, w_hi) + (dot(s_lo, w_hi) + dot(s_hi, w_lo)) + b_ref[...]


def _modulation(c, w_ada, b_ada):
    b = c.shape[0]
    n_out = w_ada.shape[1]
    return pl.pallas_call(
        _mod_kernel,
        out_shape=jax.ShapeDtypeStruct((b, n_out), F32),
        grid=(n_out // D_MODEL,),
        in_specs=[
            pl.BlockSpec((b, D_MODEL), lambda j: (0, 0)),
            pl.BlockSpec((D_MODEL, D_MODEL), lambda j: (0, j)),
            pl.BlockSpec((1, D_MODEL), lambda j: (0, j)),
        ],
        out_specs=pl.BlockSpec((b, D_MODEL), lambda j: (0, j)),
        compiler_params=_params("arbitrary"),
        name="adaln_mod",
    )(c, w_ada, b_ada.reshape(1, n_out))


_BIAS_EXT = 640


def _bias_kernel(rb_ref, o_ref):
    m = lax.broadcasted_iota(jnp.int32, (NUM_REL, _BIAS_EXT), 1)
    slot = lax.broadcasted_iota(jnp.int32, (NUM_REL, _BIAS_EXT), 0)
    idx = jnp.clip(WINDOW + CHUNK - 1 - m, -(CHUNK - 1), REL_CLIP) + (CHUNK - 1)
    onehot = jnp.where(slot == idx, 1.0, 0.0).astype(BF16)
    ext = sum(jnp.dot(piece, onehot, preferred_element_type=F32)
              for piece in _split3(rb_ref[...]))
    ext = ext * _LOG2E
    for i in range(CHUNK):
        o_ref[i] = ext[:, CHUNK - 1 - i:CHUNK - 1 - i + BAND]


def _bias_tile(rel_bias):
    out = pl.pallas_call(
        _bias_kernel,
        out_shape=jax.ShapeDtypeStruct((CHUNK, H_B, BAND), F32),
        grid=(1,),
        in_specs=[pl.BlockSpec((H_B, NUM_REL), lambda i: (0, 0))],
        out_specs=pl.BlockSpec((CHUNK, H_B, BAND), lambda i: (0, 0, 0)),
        compiler_params=_params("arbitrary"),
        name="rel_bias_tile",
    )(rel_bias)
    return out.transpose(1, 0, 2)


_SUB_ROWS = 256


def _subtiles(bb, tl):
    if bb == 1:
        step = min(tl, _SUB_ROWS)
        return [(slice(0, 1), slice(t, t + step)) for t in range(0, tl, step)]
    step = max(1, min(bb, _SUB_ROWS // tl))
    return [(slice(i, i + step), slice(0, tl)) for i in range(0, bb, step)]


def _inproj_kernel(x_ref, sc_ref, sh_ref, g_ref, lbl_ref, w_ref,
                   qa_ref, lf_ref, ka_ref, va_ref, gs_ref, qb_ref, kb_ref, vb_ref,
                   gta_ref, gtb_ref, kf_ref, vf_ref, *, n_keep):
    bb, tl, d = x_ref.shape
    j = pl.program_id(1)
    keep = j >= pl.num_programs(1) - n_keep

    lbl = lbl_ref[...]
    e = jnp.exp(lbl - jnp.max(lbl, axis=0, keepdims=True))
    lb = e[0:1] / jnp.sum(e, axis=0, keepdims=True)

    for bs, ts in _subtiles(bb, tl):
        x = x_ref[bs, ts, :]
        h = _rms(x, g_ref[...]) * (1.0 + sc_ref[bs]) + sh_ref[bs]
        hb = h.reshape(-1, d).astype(BF16)

        def proj(off, width):
            return jnp.dot(hb, w_ref[:, off:off + width], preferred_element_type=F32)

        def put(ref, val):
            ref[bs, ts, :] = val.reshape(x.shape[0], x.shape[1], -1).astype(ref.dtype)

        put(gta_ref, _sigmoid(proj(_OFF_GATE_A, D_MODEL)))
        put(gtb_ref, _sigmoid(proj(_OFF_GATE_B, D_MODEL)))
        put(qa_ref, _silu(proj(_OFF_QA, K_A)))
        f = lb + (1.0 - lb) * _sigmoid(proj(_OFF_FA, K_A))
        put(lf_ref, jnp.log(f))
        put(ka_ref, 1.0 - f)
        put(gs_ref, _silu(proj(_OFF_GA, V_A)))
        put(va_ref, proj(_OFF_IA, V_A))
        put(qb_ref, proj(_OFF_QB, W_B) * _Q_SCALE)
        kb = proj(_OFF_KB, W_B)
        vb = proj(_OFF_VB, W_B)
        put(kb_ref, kb)
        put(vb_ref, vb)

        @pl.when(keep)
        def _():
            put(kf_ref, kb)
            put(vf_ref, vb)


def _inproj(x, sc, sh, norm_g, lb_logits, w_in_bf16, bb, tl):
    b, l, d = x.shape
    nj = l // tl
    keep_rows = min(WINDOW, l)
    assert keep_rows % tl == 0
    n_keep = keep_rows // tl
    widths = [K_A, K_A, K_A, V_A, V_A, W_B, W_B, W_B, D_MODEL, D_MODEL]
    dtypes = [BF16, F32, BF16, BF16, BF16, BF16, BF16, BF16, BF16, BF16]
    tok = lambda w: pl.BlockSpec((bb, tl, w), lambda i, j: (i, j, 0))
    per_b = pl.BlockSpec((bb, 1, d), lambda i, j: (i, 0, 0))
    kept = pl.BlockSpec((bb, tl, W_B), lambda i, j: (i, jnp.maximum(j - (nj - n_keep), 0), 0))
    return pl.pallas_call(
        functools.partial(_inproj_kernel, n_keep=n_keep),
        out_shape=([jax.ShapeDtypeStruct((b, l, w), dt) for w, dt in zip(widths, dtypes)]
                   + [jax.ShapeDtypeStruct((b, keep_rows, W_B), F32)] * 2),
        grid=(b // bb, nj),
        in_specs=[tok(d), per_b, per_b, _resident((1, d)), _resident(lb_logits.shape),
                  _resident(w_in_bf16.shape)],
        out_specs=[tok(w) for w in widths] + [kept, kept],
        compiler_params=_params("arbitrary", "arbitrary"),
        name="norm_inproj",
    )(x, sc, sh, norm_g, lb_logits, w_in_bf16)


def _hgrn_kernel(qa_ref, lf_ref, ka_ref, va_ref, gs_ref, og_ref, s0_ref,
                 o_ref, sfin_ref, st_ref):
    j = pl.program_id(1)
    bb, th, _ = qa_ref.shape
    nc = th // CHUNK

    @pl.when(j == 0)
    def _():
        for bi in range(bb):
            for h in range(H_A):
                st_ref[bi, h] = s0_ref[bi, h].T

    row = lax.broadcasted_iota(jnp.int32, (th, th), 0)
    col = lax.broadcasted_iota(jnp.int32, (th, th), 1)
    tri = (row >= col) & (row // CHUNK == col // CHUNK)
    tri_bf = jnp.where(tri, 1.0, 0.0).astype(BF16)
    nt = (((1,), (1,)), ((), ()))
    tn = (((0,), (0,)), ((), ()))
    chunked = lambda a: a.reshape(nc, CHUNK, a.shape[-1])
    flat = lambda a: a.reshape(th, a.shape[-1])

    for bi in range(bb):
        cum = chunked(sum(jnp.dot(tri_bf, piece, preferred_element_type=F32)
                          for piece in _split3(lf_ref[bi])))
        mid = cum[:, CHUNK // 2:CHUNK // 2 + 1]
        tot = cum[:, CHUNK - 1:CHUNK]
        q = chunked(qa_ref[bi].astype(F32))
        k = chunked(ka_ref[bi].astype(F32))
        v = va_ref[bi].astype(BF16)
        qe = flat(q * jnp.exp(cum - mid)).astype(BF16)
        ke = flat(k * jnp.exp(mid - cum)).astype(BF16)
        qd = flat(q * jnp.exp(cum)).astype(BF16)
        kd = flat(k * jnp.exp(tot - cum)).astype(BF16)
        decay = jnp.exp(tot)
        for h in range(H_A):
            lanes = slice(h * DK_A, (h + 1) * DK_A)
            a = lax.dot_general(qe[:, lanes], ke[:, lanes], nt, preferred_element_type=F32)
            a = jnp.where(tri, a, 0.0).astype(BF16)
            intra = jnp.dot(a, v[:, lanes], preferred_element_type=F32)
            st = st_ref[bi, h]
            inter = []
            for c in range(nc):
                rows = slice(c * CHUNK, (c + 1) * CHUNK)
                inter.append(lax.dot_general(qd[rows, lanes], st.astype(BF16), nt,
                                             preferred_element_type=F32))
                st = decay[c, :, lanes] * st + lax.dot_general(
                    v[rows, lanes], kd[rows, lanes], tn, preferred_element_type=F32)
            st_ref[bi, h] = st
            o = intra + jnp.concatenate(inter, axis=0)
            o = o * lax.rsqrt(jnp.mean(o * o, axis=-1, keepdims=True) + EPS)
            gated = o * og_ref[:, lanes] * gs_ref[bi, :, lanes].astype(F32)
            o_ref[bi, :, lanes] = gated.astype(o_ref.dtype)

    @pl.when(j == pl.num_programs(1) - 1)
    def _():
        for bi in range(bb):
            for h in range(H_A):
                sfin_ref[bi, h] = st_ref[bi, h].T


def _hgrn(qa, lf, ka, va, gs, out_norm, s0, bb, th):
    b, l, _ = qa.shape
    tok = pl.BlockSpec((bb, th, K_A), lambda i, j: (i, j, 0))
    state = pl.BlockSpec((bb, H_A, DK_A, DV_A), lambda i, j: (i, 0, 0, 0))
    return pl.pallas_call(
        _hgrn_kernel,
        out_shape=[jax.ShapeDtypeStruct((b, l, V_A), BF16),
                   jax.ShapeDtypeStruct((b, H_A, DK_A, DV_A), F32)],
        grid=(b // bb, l // th),
        in_specs=[tok, tok, tok, tok, tok, _resident((1, V_A)), state],
        out_specs=[tok, state],
        scratch_shapes=[pltpu.VMEM((bb, H_A, DV_A, DK_A), F32)],
        compiler_params=_params("arbitrary", "arbitrary"),
        name="hgrn2_recurrence",
    )(qa, lf, ka, va, gs, out_norm, s0)


_HEADS_PER_UNIT = 4
_UNIT_LANES = _HEADS_PER_UNIT * DH_B
_UNIT_ROWS = _HEADS_PER_UNIT * CHUNK


def _attn_kernel(q_ref, kp_ref, kc_ref, vp_ref, vc_ref, bias_ref, o_ref, kbuf, vbuf, *,
                 mask_first_tile):
    t = pl.program_id(1)
    bb, tq, _ = q_ref.shape
    kbuf[:, 0:WINDOW, :] = kp_ref[...].astype(BF16)
    kbuf[:, WINDOW:WINDOW + tq, :] = kc_ref[...].astype(BF16)
    vbuf[:, 0:WINDOW, :] = vp_ref[...].astype(BF16)
    vbuf[:, WINDOW:WINDOW + tq, :] = vc_ref[...].astype(BF16)

    lane_head = lax.broadcasted_iota(jnp.int32, (CHUNK, _UNIT_LANES), 1) // DH_B
    own = [lane_head == h for h in range(_HEADS_PER_UNIT)]
    key = lax.broadcasted_iota(jnp.int32, (_UNIT_ROWS, BAND), 1)
    nt = (((1,), (1,)), ((), ()))

    def tile(masked):
        for bi in range(bb):
            for c in range(tq // CHUNK):
                rows = slice(c * CHUNK, (c + 1) * CHUNK)
                band = slice(c * CHUNK, c * CHUNK + BAND)
                for g in range(H_B // _HEADS_PER_UNIT):
                    lanes = slice(g * _UNIT_LANES, (g + 1) * _UNIT_LANES)
                    q4 = q_ref[bi, rows, lanes]
                    qs = jnp.concatenate([jnp.where(m, q4, jnp.zeros_like(q4)) for m in own],
                                         axis=0).astype(BF16)
                    s = lax.dot_general(qs, kbuf[bi, band, lanes], nt,
                                        preferred_element_type=F32)
                    s = s + bias_ref[g]
                    if masked:
                        s = jnp.where(key + c * CHUNK >= WINDOW, s, NEG)
                    p = jnp.exp2(s - jnp.max(s, axis=-1, keepdims=True))
                    inv = 1.0 / jnp.sum(p, axis=-1, keepdims=True)
                    o4 = jnp.dot(p.astype(BF16), vbuf[bi, band, lanes],
                                 preferred_element_type=F32) * inv
                    o = jnp.where(own[0], o4[0:CHUNK], 0.0)
                    for h in range(1, _HEADS_PER_UNIT):
                        o = jnp.where(own[h], o4[h * CHUNK:(h + 1) * CHUNK], o)
                    o_ref[bi, rows, lanes] = o.astype(o_ref.dtype)

    if mask_first_tile:
        pl.when(t == 0)(lambda: tile(True))
        pl.when(t > 0)(lambda: tile(False))
    else:
        tile(False)


def _attention(q, k, v, k_prev, v_prev, bias, bb, tq, mask_first_tile):
    b, l, _ = q.shape
    cur = pl.BlockSpec((bb, tq, W_B), lambda i, j: (i, j, 0))
    if k_prev is None:
        assert tq == WINDOW
        k_prev, v_prev = k, v
        prev = pl.BlockSpec((bb, WINDOW, W_B), lambda i, j: (i, jnp.maximum(j - 1, 0), 0))
    else:
        prev = pl.BlockSpec((bb, WINDOW, W_B), lambda i, j: (i, 0, 0))
    bias = bias.reshape(H_B // _HEADS_PER_UNIT, _UNIT_ROWS, BAND)
    return pl.pallas_call(
        functools.partial(_attn_kernel, mask_first_tile=mask_first_tile),
        out_shape=jax.ShapeDtypeStruct((b, l, W_B), BF16),
        grid=(b // bb, l // tq),
        in_specs=[cur, prev, cur, prev, cur, _resident(bias.shape)],
        out_specs=cur,
        scratch_shapes=[pltpu.VMEM((bb, WINDOW + tq, W_B), BF16),
                        pltpu.VMEM((bb, WINDOW + tq, W_B), BF16)],
        compiler_params=_params("arbitrary", "arbitrary"),
        name="band_attention",
    )(q, k_prev, k, v_prev, v, bias)


def _post_kernel(x_ref, oa_ref, ob_ref, gta_ref, gtb_ref, g1_ref, sc2_ref, sh2_ref, g2_ref,
                 nffn_ref, nfin_ref, wa_ref, wb_ref, wo_ref, wfi_ref, wfo_ref, y_ref):
    bb, tl, d = x_ref.shape
    for bs, ts in _subtiles(bb, tl):
        x = x_ref[bs, ts, :]
        n = x.shape[0] * x.shape[1]
        flat = lambda ref: ref[bs, ts, :].reshape(n, ref.shape[-1])
        per_tok = lambda v: v.reshape(x.shape)

        ya = jnp.dot(flat(oa_ref), wa_ref[...], preferred_element_type=F32)
        yb = jnp.dot(flat(ob_ref), wb_ref[...], preferred_element_type=F32)
        merged = flat(gta_ref).astype(F32) * ya + flat(gtb_ref).astype(F32) * yb
        y = jnp.dot(merged.astype(BF16), wo_ref[...], preferred_element_type=F32)
        x1 = x + g1_ref[bs] * per_tok(y)

        h2 = _rms(x1, nffn_ref[...]) * (1.0 + sc2_ref[bs]) + sh2_ref[bs]
        hb = h2.reshape(n, d).astype(BF16)
        acc = jnp.zeros((n, d), F32)
        for lo, hi in _FFN_CHUNKS:
            a = jnp.dot(hb, wfi_ref[:, lo:hi], preferred_element_type=F32)
            u = jnp.dot(hb, wfi_ref[:, D_FF + lo:D_FF + hi], preferred_element_type=F32)
            acc = acc + jnp.dot((_silu(a) * u).astype(BF16), wfo_ref[lo:hi, :],
                                preferred_element_type=F32)
        x2 = x1 + g2_ref[bs] * per_tok(acc)
        y_ref[bs, ts, :] = _rms(x2, nfin_ref[...])


def _post(x, oa, ob, gta, gtb, g1, sc2, sh2, g2, norm_ffn, norm_final,
          wa, wb, wo, wfi, wfo, bb, tl):
    b, l, d = x.shape
    tok = lambda w: pl.BlockSpec((bb, tl, w), lambda i, j: (i, j, 0))
    per_b = pl.BlockSpec((bb, 1, d), lambda i, j: (i, 0, 0))
    return pl.pallas_call(
        _post_kernel,
        out_shape=jax.ShapeDtypeStruct((b, l, d), F32),
        grid=(b // bb, l // tl),
        in_specs=[tok(d), tok(V_A), tok(W_B), tok(d), tok(d), per_b, per_b, per_b, per_b,
                  _resident((1, d)), _resident((1, d)), _resident(wa.shape),
                  _resident(wb.shape), _resident(wo.shape), _resident(wfi.shape),
                  _resident(wfo.shape)],
        out_specs=tok(d),
        compiler_params=_params("arbitrary", "arbitrary"),
        name="merge_out_ffn",
    )(x, oa, ob, gta, gtb, g1, sc2, sh2, g2, norm_ffn, norm_final, wa, wb, wo, wfi, wfo)


def _trunk(x, mod, s0, cache_k, cache_v, bias, w, *, in_tile, post_tile, hgrn_tile, attn_tile):
    b, l, d = x.shape
    sh1, sc1, g1, sh2, sc2, g2 = [m.reshape(b, 1, d) for m in jnp.split(mod, 6, axis=-1)]

    qa, lf, ka, va, gs, qb, kb, vb, gta, gtb, k_rows, v_rows = _inproj(
        x, sc1, sh1, w["norm_mix"], w["lb_logits"], w["w_in"], *in_tile)

    oa, s_fin = _hgrn(qa, lf, ka, va, gs, w["out_norm"], s0, *hgrn_tile)
    abb, atq = attn_tile
    ob = _attention(qb, kb, vb, cache_k, cache_v, bias, abb, atq, cache_k is None)

    y = _post(x, oa, ob, gta, gtb, g1, sc2, sh2, g2, w["norm_ffn"], w["norm_final"],
              w["w_branch_a"], w["w_branch_b"], w["w_out"], w["w_ffn_in"], w["w_ffn_out"],
              *post_tile)
    heads = lambda r: r.reshape(1, b, r.shape[1], H_B, DH_B)
    return y, s_fin[None], heads(k_rows), heads(v_rows)


def kernel(x_prompt, x_sample, c_prompt, c_sample, state_hgrn, cache_k, cache_v, w_ada, b_ada,
           norm_mix, w_in, hgrn_lb_logits, hgrn_out_norm, w_branch_a, rel_bias, w_branch_b,
           w_out, norm_ffn, w_ffn_in, w_ffn_out, norm_final):
    assert w_ada.shape[0] == 1, "single-layer trunk"
    w = dict(
        w_ada=w_ada[0], b_ada=b_ada[0], norm_mix=norm_mix[0].reshape(1, D_MODEL),
        lb_logits=hgrn_lb_logits, w_in=w_in[0].astype(BF16),
        out_norm=hgrn_out_norm[0].reshape(1, V_A),
        w_branch_a=w_branch_a[0].astype(BF16), w_branch_b=w_branch_b[0].astype(BF16),
        w_out=w_out[0].astype(BF16), norm_ffn=norm_ffn[0].reshape(1, D_MODEL),
        w_ffn_in=w_ffn_in[0].astype(BF16), w_ffn_out=w_ffn_out[0].astype(BF16),
        norm_final=norm_final.reshape(1, D_MODEL),
    )
    bias = _bias_tile(rel_bias[0])

    bp = x_prompt.shape[0]
    bs, ls = x_sample.shape[:2]
    mod = _modulation(jnp.concatenate([c_prompt, c_sample], axis=0), w["w_ada"], w["b_ada"])

    s0_prompt = jnp.zeros((bp, H_A, DK_A, DV_A), F32)
    y_p, s_p, k_p, v_p = _trunk(x_prompt, mod[:bp], s0_prompt, None, None, bias, w,
                                in_tile=(1, 512), post_tile=(1, 512), hgrn_tile=(bp, 256),
                                attn_tile=(1, WINDOW))
    y_s, s_s, k_s, v_s = _trunk(x_sample, mod[bp:], state_hgrn[0],
                                cache_k[0].reshape(bs, -1, W_B), cache_v[0].reshape(bs, -1, W_B),
                                bias, w, in_tile=(8, ls), post_tile=(8, ls), hgrn_tile=(8, ls),
                                attn_tile=(4, ls))
    return (y_p, y_s, s_p, k_p, v_p, s_s, k_s, v_s)
```

```python
import functools

import jax
import jax.numpy as jnp
from jax import lax
from jax.experimental import pallas as pl
from jax.experimental.pallas import tpu as pltpu

D_MODEL = 1024
CHUNK = 64
H_A, DK_A, DV_A = 4, 128, 128
K_A = H_A * DK_A
V_A = H_A * DV_A
H_B, DH_B = 8, 64
W_B = H_B * DH_B
N_PAST_CHUNKS = 8
WINDOW = N_PAST_CHUNKS * CHUNK
BAND = WINDOW + CHUNK
REL_CLIP = 128
NUM_REL = CHUNK + REL_CLIP
D_FF = 2816
IN_COLS = 4 * K_A + 3 * W_B + 2 * D_MODEL
EPS = 1e-6
NEG = -1e30
_LOG2E = 1.4426950408889634
_Q_SCALE = (DH_B ** -0.5) * _LOG2E

F32 = jnp.float32
BF16 = jnp.bfloat16

_OFF_QA, _OFF_FA, _OFF_IA, _OFF_GA = 0, K_A, 2 * K_A, 2 * K_A + V_A
_OFF_QB = 2 * K_A + 2 * V_A
_OFF_KB = _OFF_QB + W_B
_OFF_VB = _OFF_KB + W_B
_OFF_GATE_A = _OFF_VB + W_B
_OFF_GATE_B = _OFF_GATE_A + D_MODEL

_VMEM_LIMIT_BYTES = 56 * 1024 * 1024
_LANES = 128
_MXU_DIM = 256
_FFN_CHUNKS = ((0, 6 * _MXU_DIM), (6 * _MXU_DIM, D_FF))
assert D_FF % _MXU_DIM == 0


def _params(*sem):
    return pltpu.CompilerParams(dimension_semantics=sem, vmem_limit_bytes=_VMEM_LIMIT_BYTES)


def _resident(shape):
    zeros = (0,) * len(shape)
    return pl.BlockSpec(shape, lambda *_: zeros, pipeline_mode=pl.Buffered(1))


def _sigmoid(x):
    return 0.5 * jnp.tanh(0.5 * x) + 0.5


def _silu(x):
    return x * _sigmoid(x)


def _rms(x, g):
    return x * lax.rsqrt(jnp.mean(x * x, axis=-1, keepdims=True) + EPS) * g


def _split3(x):
    hi = x.astype(BF16)
    r = x - hi.astype(F32)
    mid = r.astype(BF16)
    lo = (r - mid.astype(F32)).astype(BF16)
    return hi, mid, lo


def _mod_kernel(c_ref, w_ref, b_ref, o_ref):
    s_hi, s_lo, _ = _split3(_silu(c_ref[...]))
    w_hi, w_lo, _ = _split3(w_ref[...])
    dot = lambda a, b: jnp.dot(a, b, preferred_element_type=F32)
    o_ref[...] = dot(s_hi, w_hi) + (dot(s_lo, w_hi) + dot(s_hi, w_lo)) + b_ref[...]


def _modulation(c, w_ada, b_ada):
    b = c.shape[0]
    n_out = w_ada.shape[1]
    return pl.pallas_call(
        _mod_kernel,
        out_shape=jax.ShapeDtypeStruct((b, n_out), F32),
        grid=(n_out // D_MODEL,),
        in_specs=[
            pl.BlockSpec((b, D_MODEL), lambda j: (0, 0)),
            pl.BlockSpec((D_MODEL, D_MODEL), lambda j: (0, j)),
            pl.BlockSpec((1, D_MODEL), lambda j: (0, j)),
        ],
        out_specs=pl.BlockSpec((b, D_MODEL), lambda j: (0, j)),
        compiler_params=_params("arbitrary"),
        name="adaln_mod",
    )(c, w_ada, b_ada.reshape(1, n_out))


_BIAS_EXT = 640


def _bias_kernel(rb_ref, o_ref):
    m = lax.broadcasted_iota(jnp.int32, (NUM_REL, _BIAS_EXT), 1)
    slot = lax.broadcasted_iota(jnp.int32, (NUM_REL, _BIAS_EXT), 0)
    idx = jnp.clip(WINDOW + CHUNK - 1 - m, -(CHUNK - 1), REL_CLIP) + (CHUNK - 1)
    onehot = jnp.where(slot == idx, 1.0, 0.0).astype(BF16)
    ext = sum(jnp.dot(piece, onehot, preferred_element_type=F32)
              for piece in _split3(rb_ref[...]))
    ext = ext * _LOG2E
    for i in range(CHUNK):
        o_ref[i] = ext[:, CHUNK - 1 - i:CHUNK - 1 - i + BAND]


def _bias_tile(rel_bias):
    out = pl.pallas_call(
        _bias_kernel,
        out_shape=jax.ShapeDtypeStruct((CHUNK, H_B, BAND), F32),
        grid=(1,),
        in_specs=[pl.BlockSpec((H_B, NUM_REL), lambda i: (0, 0))],
        out_specs=pl.BlockSpec((CHUNK, H_B, BAND), lambda i: (0, 0, 0)),
        compiler_params=_params("arbitrary"),
        name="rel_bias_tile",
    )(rel_bias)
    return out.transpose(1, 0, 2)


_SUB_ROWS = 256


def _subtiles(bb, tl):
    if bb == 1:
        step = min(tl, _SUB_ROWS)
        return [(slice(0, 1), slice(t, t + step)) for t in range(0, tl, step)]
    step = max(1, min(bb, _SUB_ROWS // tl))
    return [(slice(i, i + step), slice(0, tl)) for i in range(0, bb, step)]


def _inproj_kernel(x_ref, sc_ref, sh_ref, g_ref, lbl_ref, w_ref,
                   qa_ref, lf_ref, ka_ref, va_ref, gs_ref, qb_ref, kb_ref, vb_ref,
                   gta_ref, gtb_ref, kf_ref, vf_ref, *, n_keep, rows_minor):
    bb, tl, d = x_ref.shape
    j = pl.program_id(1)
    keep = j >= pl.num_programs(1) - n_keep

    lbl = lbl_ref[...]
    e = jnp.exp(lbl - jnp.max(lbl, axis=0, keepdims=True))
    lb = e[0:1] / jnp.sum(e, axis=0, keepdims=True)

    for bs, ts in _subtiles(bb, tl):
        x = x_ref[bs, ts, :]
        h = _rms(x, g_ref[...]) * (1.0 + sc_ref[bs]) + sh_ref[bs]
        hb = h.reshape(-1, d).astype(BF16)

        def proj(off, width):
            return jnp.dot(hb, w_ref[:, off:off + width], preferred_element_type=F32)

        def put(ref, val):
            ref[bs, ts, :] = val.reshape(x.shape[0], x.shape[1], -1).astype(ref.dtype)

        put(gta_ref, _sigmoid(proj(_OFF_GATE_A, D_MODEL)))
        put(gtb_ref, _sigmoid(proj(_OFF_GATE_B, D_MODEL)))
        put(qa_ref, _silu(proj(_OFF_QA, K_A)))
        f = lb + (1.0 - lb) * _sigmoid(proj(_OFF_FA, K_A))
        put(lf_ref, jnp.log(f))
        put(ka_ref, 1.0 - f)
        put(gs_ref, _silu(proj(_OFF_GA, V_A)))
        put(va_ref, proj(_OFF_IA, V_A))
        put(qb_ref, proj(_OFF_QB, W_B) * _Q_SCALE)
        kb = proj(_OFF_KB, W_B)
        vb = proj(_OFF_VB, W_B)
        put(kb_ref, kb)
        put(vb_ref, vb)

        @pl.when(keep)
        def _():
            nb, nt_ = x.shape[0], x.shape[1]
            for ref, val in ((kf_ref, kb), (vf_ref, vb)):
                if rows_minor:
                    assert nb == 1
                    ref[bs, :, :, ts] = val.T.reshape(1, H_B, DH_B, nt_)
                else:
                    for hd in range(H_B):
                        ref[bs, pl.ds(ts.start * H_B + hd, nt_, stride=H_B), :] = (
                            val[:, hd * DH_B:(hd + 1) * DH_B].reshape(nb, nt_, DH_B))


def _inproj(x, sc, sh, norm_g, lb_logits, w_in_bf16, bb, tl):
    b, l, d = x.shape
    nj = l // tl
    keep_rows = min(WINDOW, l)
    assert keep_rows % tl == 0
    n_keep = keep_rows // tl
    widths = [K_A, K_A, K_A, V_A, V_A, W_B, W_B, W_B, D_MODEL, D_MODEL]
    dtypes = [BF16, F32, BF16, BF16, BF16, BF16, BF16, BF16, BF16, BF16]
    tok = lambda w: pl.BlockSpec((bb, tl, w), lambda i, j: (i, j, 0))
    per_b = pl.BlockSpec((bb, 1, d), lambda i, j: (i, 0, 0))
    kept_block = lambda j: jnp.maximum(j - (nj - n_keep), 0)
    rows_minor = keep_rows >= _LANES
    if rows_minor:
        kept = pl.BlockSpec((bb, H_B, DH_B, tl), lambda i, j: (i, 0, 0, kept_block(j)))
        kept_shape = jax.ShapeDtypeStruct((b, H_B, DH_B, keep_rows), F32)
    else:
        kept = pl.BlockSpec((bb, tl * H_B, DH_B), lambda i, j: (i, kept_block(j), 0))
        kept_shape = jax.ShapeDtypeStruct((b, keep_rows * H_B, DH_B), F32)
    return pl.pallas_call(
        functools.partial(_inproj_kernel, n_keep=n_keep, rows_minor=rows_minor),
        out_shape=([jax.ShapeDtypeStruct((b, l, w), dt) for w, dt in zip(widths, dtypes)]
                   + [kept_shape] * 2),
        grid=(b // bb, nj),
        in_specs=[tok(d), per_b, per_b, _resident((1, d)), _resident(lb_logits.shape),
                  _resident(w_in_bf16.shape)],
        out_specs=[tok(w) for w in widths] + [kept, kept],
        compiler_params=_params("arbitrary", "arbitrary"),
        name="norm_inproj",
    )(x, sc, sh, norm_g, lb_logits, w_in_bf16)


def _hgrn_kernel(qa_ref, lf_ref, ka_ref, va_ref, gs_ref, og_ref, s0_ref,
                 o_ref, sfin_ref, st_ref):
    j = pl.program_id(1)
    bb, th, _ = qa_ref.shape
    nc = th // CHUNK

    @pl.when(j == 0)
    def _():
        for bi in range(bb):
            for h in range(H_A):
                st_ref[bi, h] = s0_ref[bi, h].T

    row = lax.broadcasted_iota(jnp.int32, (th, th), 0)
    col = lax.broadcasted_iota(jnp.int32, (th, th), 1)
    tri = (row >= col) & (row // CHUNK == col // CHUNK)
    tri_bf = jnp.where(tri, 1.0, 0.0).astype(BF16)
    nt = (((1,), (1,)), ((), ()))
    tn = (((0,), (0,)), ((), ()))
    chunked = lambda a: a.reshape(nc, CHUNK, a.shape[-1])
    flat = lambda a: a.reshape(th, a.shape[-1])

    for bi in range(bb):
        cum = chunked(sum(jnp.dot(tri_bf, piece, preferred_element_type=F32)
                          for piece in _split3(lf_ref[bi])[:2]))
        mid = cum[:, CHUNK // 2:CHUNK // 2 + 1]
        tot = cum[:, CHUNK - 1:CHUNK]
        q = chunked(qa_ref[bi].astype(F32))
        k = chunked(ka_ref[bi].astype(F32))
        v = va_ref[bi].astype(BF16)
        qe32 = q * jnp.exp(cum - mid)
        ke32 = k * jnp.exp(mid - cum)
        qe = flat(qe32).astype(BF16)
        ke = flat(ke32).astype(BF16)
        qd = flat(qe32 * jnp.exp(mid)).astype(BF16)
        kd = flat(ke32 * jnp.exp(tot - mid)).astype(BF16)
        decay = jnp.exp(tot)
        for h in range(H_A):
            lanes = slice(h * DK_A, (h + 1) * DK_A)
            a = lax.dot_general(qe[:, lanes], ke[:, lanes], nt, preferred_element_type=F32)
            a = jnp.where(tri, a, 0.0).astype(BF16)
            intra = jnp.dot(a, v[:, lanes], preferred_element_type=F32)
            st = st_ref[bi, h]
            inter = []
            for c in range(nc):
                rows = slice(c * CHUNK, (c + 1) * CHUNK)
                inter.append(lax.dot_general(qd[rows, lanes], st.astype(BF16), nt,
                                             preferred_element_type=F32))
                st = decay[c, :, lanes] * st + lax.dot_general(
                    v[rows, lanes], kd[rows, lanes], tn, preferred_element_type=F32)
            st_ref[bi, h] = st
            o = intra + jnp.concatenate(inter, axis=0)
            o = o * lax.rsqrt(jnp.mean(o * o, axis=-1, keepdims=True) + EPS)
            gated = o * og_ref[:, lanes] * gs_ref[bi, :, lanes].astype(F32)
            o_ref[bi, :, lanes] = gated.astype(o_ref.dtype)

    @pl.when(j == pl.num_programs(1) - 1)
    def _():
        for bi in range(bb):
            for h in range(H_A):
                sfin_ref[bi, h] = st_ref[bi, h].T


def _hgrn(qa, lf, ka, va, gs, out_norm, s0, bb, th):
    b, l, _ = qa.shape
    tok = pl.BlockSpec((bb, th, K_A), lambda i, j: (i, j, 0))
    state = pl.BlockSpec((bb, H_A, DK_A, DV_A), lambda i, j: (i, 0, 0, 0))
    return pl.pallas_call(
        _hgrn_kernel,
        out_shape=[jax.ShapeDtypeStruct((b, l, V_A), BF16),
                   jax.ShapeDtypeStruct((b, H_A, DK_A, DV_A), F32)],
        grid=(b // bb, l // th),
        in_specs=[tok, tok, tok, tok, tok, _resident((1, V_A)), state],
        out_specs=[tok, state],
        scratch_shapes=[pltpu.VMEM((bb, H_A, DV_A, DK_A), F32)],
        compiler_params=_params("arbitrary", "arbitrary"),
        name="hgrn2_recurrence",
    )(qa, lf, ka, va, gs, out_norm, s0)


_HEADS_PER_UNIT = 4
_UNIT_LANES = _HEADS_PER_UNIT * DH_B
_UNIT_ROWS = _HEADS_PER_UNIT * CHUNK


def _attn_kernel(q_ref, kp_ref, kc_ref, vp_ref, vc_ref, bias_ref, o_ref, kbuf, vbuf, *,
                 mask_first_tile):
    t = pl.program_id(1)
    bb, tq, _ = q_ref.shape
    for prev, cur, buf in ((kp_ref, kc_ref, kbuf), (vp_ref, vc_ref, vbuf)):
        buf[:, 0:WINDOW, :] = prev[...].astype(BF16)
        buf[:, WINDOW:WINDOW + tq, :] = cur[...].astype(BF16)

    lane_head = lax.broadcasted_iota(jnp.int32, (CHUNK, _UNIT_LANES), 1) // DH_B
    own = [lane_head == h for h in range(_HEADS_PER_UNIT)]
    key = lax.broadcasted_iota(jnp.int32, (_UNIT_ROWS, BAND), 1)
    nt = (((1,), (1,)), ((), ()))

    def tile(masked):
        for bi in range(bb):
            for c in range(tq // CHUNK):
                rows = slice(c * CHUNK, (c + 1) * CHUNK)
                band = slice(c * CHUNK, c * CHUNK + BAND)
                for g in range(H_B // _HEADS_PER_UNIT):
                    lanes = slice(g * _UNIT_LANES, (g + 1) * _UNIT_LANES)
                    q4 = q_ref[bi, rows, lanes]
                    qs = jnp.concatenate([jnp.where(m, q4, jnp.zeros_like(q4)) for m in own],
                                         axis=0).astype(BF16)
                    s = lax.dot_general(qs, kbuf[bi, band, lanes], nt,
                                        preferred_element_type=F32)
                    s = s + bias_ref[g]
                    if masked:
                        s = jnp.where(key + c * CHUNK >= WINDOW, s, NEG)
                    p = jnp.exp2(s - jnp.max(s, axis=-1, keepdims=True))
                    inv = 1.0 / jnp.sum(p, axis=-1, keepdims=True)
                    o4 = jnp.dot(p.astype(BF16), vbuf[bi, band, lanes],
                                 preferred_element_type=F32) * inv
                    o = jnp.where(own[0], o4[0:CHUNK], 0.0)
                    for h in range(1, _HEADS_PER_UNIT):
                        o = jnp.where(own[h], o4[h * CHUNK:(h + 1) * CHUNK], o)
                    o_ref[bi, rows, lanes] = o.astype(o_ref.dtype)

    if mask_first_tile:
        pl.when(t == 0)(lambda: tile(True))
        pl.when(t > 0)(lambda: tile(False))
    else:
        tile(False)


def _attn_cache_kernel(q_ref, kt_ref, kc_ref, vt_ref, vc_ref, bias_ref, o_ref):
    bb = q_ref.shape[0]
    lane_head = lax.broadcasted_iota(jnp.int32, (CHUNK, _UNIT_LANES), 1) // DH_B
    own = [lane_head == h for h in range(_HEADS_PER_UNIT)]
    nt = (((1,), (1,)), ((), ()))
    dot = lambda a, b: jnp.dot(a, b, preferred_element_type=F32)
    dot_nt = lambda a, b: lax.dot_general(a, b, nt, preferred_element_type=F32)

    for bi in range(bb):
        for g in range(H_B // _HEADS_PER_UNIT):
            lanes = slice(g * _UNIT_LANES, (g + 1) * _UNIT_LANES)
            heads = slice(g * _HEADS_PER_UNIT, (g + 1) * _HEADS_PER_UNIT)
            kt4 = kt_ref[bi, heads].reshape(_UNIT_LANES, WINDOW).astype(BF16)
            vt4 = vt_ref[bi, heads].reshape(_UNIT_LANES, WINDOW).astype(BF16)
            q4 = q_ref[bi, :, lanes]
            qs = jnp.concatenate([jnp.where(m, q4, jnp.zeros_like(q4)) for m in own],
                                 axis=0).astype(BF16)
            s_old = dot(qs, kt4) + bias_ref[g, :, 0:WINDOW]
            s_new = dot_nt(qs, kc_ref[bi, :, lanes]) + bias_ref[g, :, WINDOW:BAND]
            top = jnp.maximum(jnp.max(s_old, axis=-1, keepdims=True),
                              jnp.max(s_new, axis=-1, keepdims=True))
            p_old = jnp.exp2(s_old - top)
            p_new = jnp.exp2(s_new - top)
            inv = 1.0 / (jnp.sum(p_old, axis=-1, keepdims=True)
                         + jnp.sum(p_new, axis=-1, keepdims=True))
            o4 = (dot_nt(p_old.astype(BF16), vt4)
                  + dot(p_new.astype(BF16), vc_ref[bi, :, lanes])) * inv
            o = jnp.where(own[0], o4[0:CHUNK], 0.0)
            for h in range(1, _HEADS_PER_UNIT):
                o = jnp.where(own[h], o4[h * CHUNK:(h + 1) * CHUNK], o)
            o_ref[bi, :, lanes] = o.astype(o_ref.dtype)


def _attention(q, k, v, cache_kt, cache_vt, bias, bb, tq):
    b, l, _ = q.shape
    cur = pl.BlockSpec((bb, tq, W_B), lambda i, j: (i, j, 0))
    bias = bias.reshape(H_B // _HEADS_PER_UNIT, _UNIT_ROWS, BAND)
    out_shape = jax.ShapeDtypeStruct((b, l, W_B), BF16)
    if cache_kt is None:
        assert tq == WINDOW
        prev = pl.BlockSpec((bb, WINDOW, W_B), lambda i, j: (i, jnp.maximum(j - 1, 0), 0))
        return pl.pallas_call(
            functools.partial(_attn_kernel, mask_first_tile=True),
            out_shape=out_shape,
            grid=(b // bb, l // tq),
            in_specs=[cur, prev, cur, prev, cur, _resident(bias.shape)],
            out_specs=cur,
            scratch_shapes=[pltpu.VMEM((bb, WINDOW + tq, W_B), BF16),
                            pltpu.VMEM((bb, WINDOW + tq, W_B), BF16)],
            compiler_params=_params("arbitrary", "arbitrary"),
            name="band_attention",
        )(q, k, k, v, v, bias)
    assert tq == l == CHUNK and cache_kt.shape[1:] == (H_B, DH_B, WINDOW)
    cached = pl.BlockSpec((bb, H_B, DH_B, WINDOW), lambda i, j: (i, 0, 0, 0))
    return pl.pallas_call(
        _attn_cache_kernel,
        out_shape=out_shape,
        grid=(b // bb, 1),
        in_specs=[cur, cached, cur, cached, cur, _resident(bias.shape)],
        out_specs=cur,
        compiler_params=_params("arbitrary", "arbitrary"),
        name="cache_attention",
    )(q, cache_kt, k, cache_vt, v, bias)


def _post_kernel(x_ref, oa_ref, ob_ref, gta_ref, gtb_ref, g1_ref, sc2_ref, sh2_ref, g2_ref,
                 nffn_ref, nfin_ref, wa_ref, wb_ref, wo_ref, wfi_ref, wfo_ref, y_ref):
    bb, tl, d = x_ref.shape
    for bs, ts in _subtiles(bb, tl):
        x = x_ref[bs, ts, :]
        n = x.shape[0] * x.shape[1]
        flat = lambda ref: ref[bs, ts, :].reshape(n, ref.shape[-1])
        per_tok = lambda v: v.reshape(x.shape)

        ya = jnp.dot(flat(oa_ref), wa_ref[...], preferred_element_type=F32)
        yb = jnp.dot(flat(ob_ref), wb_ref[...], preferred_element_type=F32)
        merged = flat(gta_ref).astype(F32) * ya + flat(gtb_ref).astype(F32) * yb
        y = jnp.dot(merged.astype(BF16), wo_ref[...], preferred_element_type=F32)
        x1 = x + g1_ref[bs] * per_tok(y)

        h2 = _rms(x1, nffn_ref[...]) * (1.0 + sc2_ref[bs]) + sh2_ref[bs]
        hb = h2.reshape(n, d).astype(BF16)
        acc = jnp.zeros((n, d), F32)
        for lo, hi in _FFN_CHUNKS:
            a = jnp.dot(hb, wfi_ref[:, lo:hi], preferred_element_type=F32)
            u = jnp.dot(hb, wfi_ref[:, D_FF + lo:D_FF + hi], preferred_element_type=F32)
            acc = acc + jnp.dot((_silu(a) * u).astype(BF16), wfo_ref[lo:hi, :],
                                preferred_element_type=F32)
        x2 = x1 + g2_ref[bs] * per_tok(acc)
        y_ref[bs, ts, :] = _rms(x2, nfin_ref[...])


def _post(x, oa, ob, gta, gtb, g1, sc2, sh2, g2, norm_ffn, norm_final,
          wa, wb, wo, wfi, wfo, bb, tl):
    b, l, d = x.shape
    tok = lambda w: pl.BlockSpec((bb, tl, w), lambda i, j: (i, j, 0))
    per_b = pl.BlockSpec((bb, 1, d), lambda i, j: (i, 0, 0))
    return pl.pallas_call(
        _post_kernel,
        out_shape=jax.ShapeDtypeStruct((b, l, d), F32),
        grid=(b // bb, l // tl),
        in_specs=[tok(d), tok(V_A), tok(W_B), tok(d), tok(d), per_b, per_b, per_b, per_b,
                  _resident((1, d)), _resident((1, d)), _resident(wa.shape),
                  _resident(wb.shape), _resident(wo.shape), _resident(wfi.shape),
                  _resident(wfo.shape)],
        out_specs=tok(d),
        compiler_params=_params("arbitrary", "arbitrary"),
        name="merge_out_ffn",
    )(x, oa, ob, gta, gtb, g1, sc2, sh2, g2, norm_ffn, norm_final, wa, wb, wo, wfi, wfo)


def _trunk(x, mod, s0, cache_k, cache_v, bias, w, *, in_tile, post_tile, hgrn_tile, attn_tile):
    b, l, d = x.shape
    sh1, sc1, g1, sh2, sc2, g2 = [m.reshape(b, 1, d) for m in jnp.split(mod, 6, axis=-1)]

    qa, lf, ka, va, gs, qb, kb, vb, gta, gtb, k_rows, v_rows = _inproj(
        x, sc1, sh1, w["norm_mix"], w["lb_logits"], w["w_in"], *in_tile)

    oa, s_fin = _hgrn(qa, lf, ka, va, gs, w["out_norm"], s0, *hgrn_tile)
    ob = _attention(qb, kb, vb, cache_k, cache_v, bias, *attn_tile)

    y = _post(x, oa, ob, gta, gtb, g1, sc2, sh2, g2, w["norm_ffn"], w["norm_final"],
              w["w_branch_a"], w["w_branch_b"], w["w_out"], w["w_ffn_in"], w["w_ffn_out"],
              *post_tile)
    def rows_first(r):
        if r.ndim == 4:
            return r.transpose(0, 3, 1, 2)[None]
        return r.reshape(1, b, r.shape[1] // H_B, H_B, DH_B)

    return y, s_fin[None], rows_first(k_rows), rows_first(v_rows)


def kernel(x_prompt, x_sample, c_prompt, c_sample, state_hgrn, cache_k, cache_v, w_ada, b_ada,
           norm_mix, w_in, hgrn_lb_logits, hgrn_out_norm, w_branch_a, rel_bias, w_branch_b,
           w_out, norm_ffn, w_ffn_in, w_ffn_out, norm_final):
    assert w_ada.shape[0] == 1, "single-layer trunk"
    w = dict(
        w_ada=w_ada[0], b_ada=b_ada[0], norm_mix=norm_mix[0].reshape(1, D_MODEL),
        lb_logits=hgrn_lb_logits, w_in=w_in[0].astype(BF16),
        out_norm=hgrn_out_norm[0].reshape(1, V_A),
        w_branch_a=w_branch_a[0].astype(BF16), w_branch_b=w_branch_b[0].astype(BF16),
        w_out=w_out[0].astype(BF16), norm_ffn=norm_ffn[0].reshape(1, D_MODEL),
        w_ffn_in=w_ffn_in[0].astype(BF16), w_ffn_out=w_ffn_out[0].astype(BF16),
        norm_final=norm_final.reshape(1, D_MODEL),
    )
    bias = _bias_tile(rel_bias[0])

    bp = x_prompt.shape[0]
    bs, ls = x_sample.shape[:2]
    mod = _modulation(jnp.concatenate([c_prompt, c_sample], axis=0), w["w_ada"], w["b_ada"])

    s0_prompt = jnp.zeros((bp, H_A, DK_A, DV_A), F32)
    y_p, s_p, k_p, v_p = _trunk(x_prompt, mod[:bp], s0_prompt, None, None, bias, w,
                                in_tile=(1, 512), post_tile=(1, 512), hgrn_tile=(bp, 256),
                                attn_tile=(1, WINDOW))
    y_s, s_s, k_s, v_s = _trunk(x_sample, mod[bp:], state_hgrn[0],
                                cache_k[0].transpose(0, 2, 3, 1), cache_v[0].transpose(0, 2, 3, 1),
                                bias, w, in_tile=(8, ls), post_tile=(8, ls), hgrn_tile=(8, ls),
                                attn_tile=(4, ls))
    return (y_p, y_s, s_p, k_p, v_p, s_s, k_s, v_s)
```

```python
import functools

import jax
import jax.numpy as jnp
from jax import lax
from jax.experimental import pallas as pl
from jax.experimental.pallas import tpu as pltpu

D_MODEL = 1024
CHUNK = 64
H_A, DK_A, DV_A = 4, 128, 128
K_A = H_A * DK_A
V_A = H_A * DV_A
H_B, DH_B = 8, 64
W_B = H_B * DH_B
N_PAST_CHUNKS = 8
WINDOW = N_PAST_CHUNKS * CHUNK
BAND = WINDOW + CHUNK
REL_CLIP = 128
NUM_REL = CHUNK + REL_CLIP
D_FF = 2816
IN_COLS = 4 * K_A + 3 * W_B + 2 * D_MODEL
EPS = 1e-6
NEG = -1e30
_LOG2E = 1.4426950408889634
_Q_SCALE = (DH_B ** -0.5) * _LOG2E

F32 = jnp.float32
BF16 = jnp.bfloat16

_OFF_QA, _OFF_FA, _OFF_IA, _OFF_GA = 0, K_A, 2 * K_A, 2 * K_A + V_A
_OFF_QB = 2 * K_A + 2 * V_A
_OFF_KB = _OFF_QB + W_B
_OFF_VB = _OFF_KB + W_B
_OFF_GATE_A = _OFF_VB + W_B
_OFF_GATE_B = _OFF_GATE_A + D_MODEL

_VMEM_LIMIT_BYTES = 56 * 1024 * 1024
_LANES = 128
_MXU_DIM = 256
_FFN_CHUNKS = ((0, 6 * _MXU_DIM), (6 * _MXU_DIM, D_FF))
assert D_FF % _MXU_DIM == 0


def _params(*sem):
    return pltpu.CompilerParams(dimension_semantics=sem, vmem_limit_bytes=_VMEM_LIMIT_BYTES)


def _resident(shape):
    zeros = (0,) * len(shape)
    return pl.BlockSpec(shape, lambda *_: zeros, pipeline_mode=pl.Buffered(1))


def _sigmoid(x):
    return 0.5 * jnp.tanh(0.5 * x) + 0.5


def _silu(x):
    return x * _sigmoid(x)


def _rms(x, g):
    return x * lax.rsqrt(jnp.mean(x * x, axis=-1, keepdims=True) + EPS) * g


def _split3(x):
    hi = x.astype(BF16)
    r = x - hi.astype(F32)
    mid = r.astype(BF16)
    lo = (r - mid.astype(F32)).astype(BF16)
    return hi, mid, lo


def _mod_kernel(c_ref, w_ref, b_ref, o_ref):
    s_hi, s_lo, _ = _split3(_silu(c_ref[...]))
    w_hi, w_lo, _ = _split3(w_ref[...])
    dot = lambda a, b: jnp.dot(a, b, preferred_element_type=F32)
    o_ref[...] = dot(s_hi, w_hi) + (dot(s_lo, w_hi) + dot(s_hi, w_lo)) + b_ref[...]


def _modulation(c, w_ada, b_ada):
    b = c.shape[0]
    n_out = w_ada.shape[1]
    return pl.pallas_call(
        _mod_kernel,
        out_shape=jax.ShapeDtypeStruct((b, n_out), F32),
        grid=(n_out // D_MODEL,),
        in_specs=[
            pl.BlockSpec((b, D_MODEL), lambda j: (0, 0)),
            pl.BlockSpec((D_MODEL, D_MODEL), lambda j: (0, j)),
            pl.BlockSpec((1, D_MODEL), lambda j: (0, j)),
        ],
        out_specs=pl.BlockSpec((b, D_MODEL), lambda j: (0, j)),
        compiler_params=_params("arbitrary"),
        name="adaln_mod",
    )(c, w_ada, b_ada.reshape(1, n_out))


_BIAS_EXT = 640


def _bias_kernel(rb_ref, o_ref):
    m = lax.broadcasted_iota(jnp.int32, (NUM_REL, _BIAS_EXT), 1)
    slot = lax.broadcasted_iota(jnp.int32, (NUM_REL, _BIAS_EXT), 0)
    idx = jnp.clip(WINDOW + CHUNK - 1 - m, -(CHUNK - 1), REL_CLIP) + (CHUNK - 1)
    onehot = jnp.where(slot == idx, 1.0, 0.0).astype(BF16)
    ext = sum(jnp.dot(piece, onehot, preferred_element_type=F32)
              for piece in _split3(rb_ref[...]))
    ext = ext * _LOG2E
    for i in range(CHUNK):
        o_ref[i] = ext[:, CHUNK - 1 - i:CHUNK - 1 - i + BAND]


def _bias_tile(rel_bias):
    out = pl.pallas_call(
        _bias_kernel,
        out_shape=jax.ShapeDtypeStruct((CHUNK, H_B, BAND), F32),
        grid=(1,),
        in_specs=[pl.BlockSpec((H_B, NUM_REL), lambda i: (0, 0))],
        out_specs=pl.BlockSpec((CHUNK, H_B, BAND), lambda i: (0, 0, 0)),
        compiler_params=_params("arbitrary"),
        name="rel_bias_tile",
    )(rel_bias)
    return out.transpose(1, 0, 2)


_SUB_ROWS = 256


def _subtiles(bb, tl):
    if bb == 1:
        step = min(tl, _SUB_ROWS)
        return [(slice(0, 1), slice(t, t + step)) for t in range(0, tl, step)]
    step = max(1, min(bb, _SUB_ROWS // tl))
    return [(slice(i, i + step), slice(0, tl)) for i in range(0, bb, step)]


def _trace_skewed(tiles, lead):
    tiles = list(tiles)
    started, live = 0, []
    while started < len(tiles) or live:
        if started < len(tiles) and (not live or live[-1][1] >= lead):
            live.append([tiles[started], 0])
            started += 1
        for entry in list(live):
            try:
                next(entry[0])
                entry[1] += 1
            except StopIteration:
                live.remove(entry)


def _inproj_kernel(x_ref, sc_ref, sh_ref, g_ref, lbl_ref, w_ref,
                   qa_ref, lf_ref, ka_ref, va_ref, gs_ref, qb_ref, kb_ref, vb_ref,
                   gta_ref, gtb_ref, kf_ref, vf_ref, *, n_keep, rows_minor):
    bb, tl, d = x_ref.shape
    j = pl.program_id(1)
    keep = j >= pl.num_programs(1) - n_keep

    lbl = lbl_ref[...]
    e = jnp.exp(lbl - jnp.max(lbl, axis=0, keepdims=True))
    lb = e[0:1] / jnp.sum(e, axis=0, keepdims=True)

    def tile(bs, ts):
        x = x_ref[bs, ts, :]
        h = _rms(x, g_ref[...]) * (1.0 + sc_ref[bs]) + sh_ref[bs]
        hb = h.reshape(-1, d).astype(BF16)
        yield

        def proj(off, width):
            return jnp.dot(hb, w_ref[:, off:off + width], preferred_element_type=F32)

        def put(ref, val):
            ref[bs, ts, :] = val.reshape(x.shape[0], x.shape[1], -1).astype(ref.dtype)

        put(gta_ref, _sigmoid(proj(_OFF_GATE_A, D_MODEL)))
        yield
        put(gtb_ref, _sigmoid(proj(_OFF_GATE_B, D_MODEL)))
        yield
        put(qa_ref, _silu(proj(_OFF_QA, K_A)))
        f = lb + (1.0 - lb) * _sigmoid(proj(_OFF_FA, K_A))
        put(lf_ref, jnp.log(f))
        put(ka_ref, 1.0 - f)
        yield
        put(gs_ref, _silu(proj(_OFF_GA, V_A)))
        put(va_ref, proj(_OFF_IA, V_A))
        yield
        put(qb_ref, proj(_OFF_QB, W_B) * _Q_SCALE)
        kb = proj(_OFF_KB, W_B)
        vb = proj(_OFF_VB, W_B)
        put(kb_ref, kb)
        put(vb_ref, vb)
        cache_rows.append((bs, ts, kb, vb))
        yield

    cache_rows = []
    _trace_skewed([tile(bs, ts) for bs, ts in _subtiles(bb, tl)], lead=2)

    @pl.when(keep)
    def _():
        for bs, ts, kb, vb in cache_rows:
            nb, nt_ = bs.stop - bs.start, ts.stop - ts.start
            for ref, val in ((kf_ref, kb), (vf_ref, vb)):
                if rows_minor:
                    assert nb == 1
                    ref[bs, :, :, ts] = val.T.reshape(1, H_B, DH_B, nt_)
                else:
                    for hd in range(H_B):
                        ref[bs, pl.ds(ts.start * H_B + hd, nt_, stride=H_B), :] = (
                            val[:, hd * DH_B:(hd + 1) * DH_B].reshape(nb, nt_, DH_B))


def _inproj(x, sc, sh, norm_g, lb_logits, w_in_bf16, bb, tl):
    b, l, d = x.shape
    nj = l // tl
    keep_rows = min(WINDOW, l)
    assert keep_rows % tl == 0
    n_keep = keep_rows // tl
    widths = [K_A, K_A, K_A, V_A, V_A, W_B, W_B, W_B, D_MODEL, D_MODEL]
    dtypes = [BF16, F32, BF16, BF16, BF16, BF16, BF16, BF16, BF16, BF16]
    tok = lambda w: pl.BlockSpec((bb, tl, w), lambda i, j: (i, j, 0))
    per_b = pl.BlockSpec((bb, 1, d), lambda i, j: (i, 0, 0))
    kept_block = lambda j: jnp.maximum(j - (nj - n_keep), 0)
    rows_minor = keep_rows >= _LANES
    if rows_minor:
        kept = pl.BlockSpec((bb, H_B, DH_B, tl), lambda i, j: (i, 0, 0, kept_block(j)))
        kept_shape = jax.ShapeDtypeStruct((b, H_B, DH_B, keep_rows), F32)
    else:
        kept = pl.BlockSpec((bb, tl * H_B, DH_B), lambda i, j: (i, kept_block(j), 0))
        kept_shape = jax.ShapeDtypeStruct((b, keep_rows * H_B, DH_B), F32)
    return pl.pallas_call(
        functools.partial(_inproj_kernel, n_keep=n_keep, rows_minor=rows_minor),
        out_shape=([jax.ShapeDtypeStruct((b, l, w), dt) for w, dt in zip(widths, dtypes)]
                   + [kept_shape] * 2),
        grid=(b // bb, nj),
        in_specs=[tok(d), per_b, per_b, _resident((1, d)), _resident(lb_logits.shape),
                  _resident(w_in_bf16.shape)],
        out_specs=[tok(w) for w in widths] + [kept, kept],
        compiler_params=_params("arbitrary", "arbitrary"),
        name="norm_inproj",
    )(x, sc, sh, norm_g, lb_logits, w_in_bf16)


_HGRN_TILE = 256


def _hgrn_kernel(qa_ref, lf_ref, ka_ref, va_ref, gs_ref, og_ref, s0_ref,
                 o_ref, sfin_ref, st_ref):
    j = pl.program_id(1)
    bb, tb, _ = qa_ref.shape
    th = min(tb, _HGRN_TILE)
    nc = th // CHUNK

    @pl.when(j == 0)
    def _():
        for bi in range(bb):
            for h in range(H_A):
                st_ref[bi, h] = s0_ref[bi, h].T

    row = lax.broadcasted_iota(jnp.int32, (th, th), 0)
    col = lax.broadcasted_iota(jnp.int32, (th, th), 1)
    tri = (row >= col) & (row // CHUNK == col // CHUNK)
    tri_bf = jnp.where(tri, 1.0, 0.0).astype(BF16)
    nt = (((1,), (1,)), ((), ()))
    tn = (((0,), (0,)), ((), ()))
    chunked = lambda a: a.reshape(nc, CHUNK, a.shape[-1])
    flat = lambda a: a.reshape(th, a.shape[-1])

    for t0, bi in [(t0, bi) for t0 in range(0, tb, th) for bi in range(bb)]:
        tile = slice(t0, t0 + th)
        cum = chunked(sum(jnp.dot(tri_bf, piece, preferred_element_type=F32)
                          for piece in _split3(lf_ref[bi, tile, :])[:2]))
        mid = cum[:, CHUNK // 2:CHUNK // 2 + 1]
        tot = cum[:, CHUNK - 1:CHUNK]
        q = chunked(qa_ref[bi, tile, :].astype(F32))
        k = chunked(ka_ref[bi, tile, :].astype(F32))
        v = va_ref[bi, tile, :].astype(BF16)
        qe32 = q * jnp.exp(cum - mid)
        ke32 = k * jnp.exp(mid - cum)
        qe = flat(qe32).astype(BF16)
        ke = flat(ke32).astype(BF16)
        qd = flat(qe32 * jnp.exp(mid)).astype(BF16)
        kd = flat(ke32 * jnp.exp(tot - mid)).astype(BF16)
        decay = jnp.exp(tot)
        for h in range(H_A):
            lanes = slice(h * DK_A, (h + 1) * DK_A)
            a = lax.dot_general(qe[:, lanes], ke[:, lanes], nt, preferred_element_type=F32)
            a = jnp.where(tri, a, 0.0).astype(BF16)
            intra = jnp.dot(a, v[:, lanes], preferred_element_type=F32)
            st = st_ref[bi, h]
            inter = []
            for c in range(nc):
                rows = slice(c * CHUNK, (c + 1) * CHUNK)
                inter.append(lax.dot_general(qd[rows, lanes], st.astype(BF16), nt,
                                             preferred_element_type=F32))
                st = decay[c, :, lanes] * st + lax.dot_general(
                    v[rows, lanes], kd[rows, lanes], tn, preferred_element_type=F32)
            st_ref[bi, h] = st
            o = intra + jnp.concatenate(inter, axis=0)
            o = o * lax.rsqrt(jnp.mean(o * o, axis=-1, keepdims=True) + EPS)
            gated = o * og_ref[:, lanes] * gs_ref[bi, tile, lanes].astype(F32)
            o_ref[bi, tile, lanes] = gated.astype(o_ref.dtype)

    @pl.when(j == pl.num_programs(1) - 1)
    def _():
        for bi in range(bb):
            for h in range(H_A):
                sfin_ref[bi, h] = st_ref[bi, h].T


def _hgrn(qa, lf, ka, va, gs, out_norm, s0, bb, th):
    b, l, _ = qa.shape
    tok = pl.BlockSpec((bb, th, K_A), lambda i, j: (i, j, 0))
    state = pl.BlockSpec((bb, H_A, DK_A, DV_A), lambda i, j: (i, 0, 0, 0))
    return pl.pallas_call(
        _hgrn_kernel,
        out_shape=[jax.ShapeDtypeStruct((b, l, V_A), BF16),
                   jax.ShapeDtypeStruct((b, H_A, DK_A, DV_A), F32)],
        grid=(b // bb, l // th),
        in_specs=[tok, tok, tok, tok, tok, _resident((1, V_A)), state],
        out_specs=[tok, state],
        scratch_shapes=[pltpu.VMEM((bb, H_A, DV_A, DK_A), F32)],
        compiler_params=_params("arbitrary", "arbitrary"),
        name="hgrn2_recurrence",
    )(qa, lf, ka, va, gs, out_norm, s0)


_HEADS_PER_UNIT = 4
_UNIT_LANES = _HEADS_PER_UNIT * DH_B
_UNIT_ROWS = _HEADS_PER_UNIT * CHUNK


def _attn_kernel(q_ref, kp_ref, kc_ref, vp_ref, vc_ref, bias_ref, o_ref, kbuf, vbuf, *,
                 mask_first_tile):
    t = pl.program_id(1)
    bb, tq, _ = q_ref.shape
    for prev, cur, buf in ((kp_ref, kc_ref, kbuf), (vp_ref, vc_ref, vbuf)):
        buf[:, 0:WINDOW, :] = prev[...].astype(BF16)
        buf[:, WINDOW:WINDOW + tq, :] = cur[...].astype(BF16)

    lane_head = lax.broadcasted_iota(jnp.int32, (CHUNK, _UNIT_LANES), 1) // DH_B
    own = [lane_head == h for h in range(_HEADS_PER_UNIT)]
    key = lax.broadcasted_iota(jnp.int32, (_UNIT_ROWS, BAND), 1)
    nt = (((1,), (1,)), ((), ()))

    def tile(masked):
        for bi in range(bb):
            for c in range(tq // CHUNK):
                rows = slice(c * CHUNK, (c + 1) * CHUNK)
                band = slice(c * CHUNK, c * CHUNK + BAND)
                for g in range(H_B // _HEADS_PER_UNIT):
                    lanes = slice(g * _UNIT_LANES, (g + 1) * _UNIT_LANES)
                    q4 = q_ref[bi, rows, lanes]
                    qs = jnp.concatenate([jnp.where(m, q4, jnp.zeros_like(q4)) for m in own],
                                         axis=0).astype(BF16)
                    s = lax.dot_general(qs, kbuf[bi, band, lanes], nt,
                                        preferred_element_type=F32)
                    s = s + bias_ref[g]
                    if masked:
                        s = jnp.where(key + c * CHUNK >= WINDOW, s, NEG)
                    p = jnp.exp2(s - jnp.max(s, axis=-1, keepdims=True))
                    inv = 1.0 / jnp.sum(p, axis=-1, keepdims=True)
                    o4 = jnp.dot(p.astype(BF16), vbuf[bi, band, lanes],
                                 preferred_element_type=F32) * inv
                    o = jnp.where(own[0], o4[0:CHUNK], 0.0)
                    for h in range(1, _HEADS_PER_UNIT):
                        o = jnp.where(own[h], o4[h * CHUNK:(h + 1) * CHUNK], o)
                    o_ref[bi, rows, lanes] = o.astype(o_ref.dtype)

    if mask_first_tile:
        pl.when(t == 0)(lambda: tile(True))
        pl.when(t > 0)(lambda: tile(False))
    else:
        tile(False)


def _attn_cache_kernel(q_ref, kt_ref, kc_ref, vt_ref, vc_ref, bias_ref, o_ref):
    bb = q_ref.shape[0]
    lane_head = lax.broadcasted_iota(jnp.int32, (CHUNK, _UNIT_LANES), 1) // DH_B
    own = [lane_head == h for h in range(_HEADS_PER_UNIT)]
    nt = (((1,), (1,)), ((), ()))
    dot = lambda a, b: jnp.dot(a, b, preferred_element_type=F32)
    dot_nt = lambda a, b: lax.dot_general(a, b, nt, preferred_element_type=F32)

    for bi in range(bb):
        for g in range(H_B // _HEADS_PER_UNIT):
            lanes = slice(g * _UNIT_LANES, (g + 1) * _UNIT_LANES)
            heads = slice(g * _HEADS_PER_UNIT, (g + 1) * _HEADS_PER_UNIT)
            kt4 = kt_ref[bi, heads].reshape(_UNIT_LANES, WINDOW).astype(BF16)
            vt4 = vt_ref[bi, heads].reshape(_UNIT_LANES, WINDOW).astype(BF16)
            q4 = q_ref[bi, :, lanes]
            qs = jnp.concatenate([jnp.where(m, q4, jnp.zeros_like(q4)) for m in own],
                                 axis=0).astype(BF16)
            s_old = dot(qs, kt4) + bias_ref[g, :, 0:WINDOW]
            s_new = dot_nt(qs, kc_ref[bi, :, lanes]) + bias_ref[g, :, WINDOW:BAND]
            top = jnp.maximum(jnp.max(s_old, axis=-1, keepdims=True),
                              jnp.max(s_new, axis=-1, keepdims=True))
            p_old = jnp.exp2(s_old - top)
            p_new = jnp.exp2(s_new - top)
            inv = 1.0 / (jnp.sum(p_old, axis=-1, keepdims=True)
                         + jnp.sum(p_new, axis=-1, keepdims=True))
            o4 = (dot_nt(p_old.astype(BF16), vt4)
                  + dot(p_new.astype(BF16), vc_ref[bi, :, lanes])) * inv
            o = jnp.where(own[0], o4[0:CHUNK], 0.0)
            for h in range(1, _HEADS_PER_UNIT):
                o = jnp.where(own[h], o4[h * CHUNK:(h + 1) * CHUNK], o)
            o_ref[bi, :, lanes] = o.astype(o_ref.dtype)


def _attention(q, k, v, cache_kt, cache_vt, bias, bb, tq):
    b, l, _ = q.shape
    cur = pl.BlockSpec((bb, tq, W_B), lambda i, j: (i, j, 0))
    bias = bias.reshape(H_B // _HEADS_PER_UNIT, _UNIT_ROWS, BAND)
    out_shape = jax.ShapeDtypeStruct((b, l, W_B), BF16)
    if cache_kt is None:
        assert tq % WINDOW == 0
        per_tile = tq // WINDOW
        prev = pl.BlockSpec((bb, WINDOW, W_B),
                            lambda i, j: (i, jnp.maximum(j * per_tile - 1, 0), 0))
        return pl.pallas_call(
            functools.partial(_attn_kernel, mask_first_tile=True),
            out_shape=out_shape,
            grid=(b // bb, l // tq),
            in_specs=[cur, prev, cur, prev, cur, _resident(bias.shape)],
            out_specs=cur,
            scratch_shapes=[pltpu.VMEM((bb, WINDOW + tq, W_B), BF16),
                            pltpu.VMEM((bb, WINDOW + tq, W_B), BF16)],
            compiler_params=_params("arbitrary", "arbitrary"),
            name="band_attention",
        )(q, k, k, v, v, bias)
    assert tq == l == CHUNK and cache_kt.shape[1:] == (H_B, DH_B, WINDOW)
    cached = pl.BlockSpec((bb, H_B, DH_B, WINDOW), lambda i, j: (i, 0, 0, 0))
    return pl.pallas_call(
        _attn_cache_kernel,
        out_shape=out_shape,
        grid=(b // bb, 1),
        in_specs=[cur, cached, cur, cached, cur, _resident(bias.shape)],
        out_specs=cur,
        compiler_params=_params("arbitrary", "arbitrary"),
        name="cache_attention",
    )(q, cache_kt, k, cache_vt, v, bias)


def _post_kernel(x_ref, oa_ref, ob_ref, gta_ref, gtb_ref, g1_ref, sc2_ref, sh2_ref, g2_ref,
                 nffn_ref, nfin_ref, wa_ref, wb_ref, wo_ref, wfi_ref, wfo_ref, y_ref):
    bb, tl, d = x_ref.shape

    def tile(bs, ts):
        x = x_ref[bs, ts, :]
        n = x.shape[0] * x.shape[1]
        flat = lambda ref: ref[bs, ts, :].reshape(n, ref.shape[-1])
        per_tok = lambda v: v.reshape(x.shape)

        ya = jnp.dot(flat(oa_ref), wa_ref[...], preferred_element_type=F32)
        yb = jnp.dot(flat(ob_ref), wb_ref[...], preferred_element_type=F32)
        merged = flat(gta_ref).astype(F32) * ya + flat(gtb_ref).astype(F32) * yb
        yield
        y = jnp.dot(merged.astype(BF16), wo_ref[...], preferred_element_type=F32)
        x1 = x + g1_ref[bs] * per_tok(y)
        h2 = _rms(x1, nffn_ref[...]) * (1.0 + sc2_ref[bs]) + sh2_ref[bs]
        hb = h2.reshape(n, d).astype(BF16)
        yield
        acc = jnp.zeros((n, d), F32)
        for lo, hi in _FFN_CHUNKS:
            a = jnp.dot(hb, wfi_ref[:, lo:hi], preferred_element_type=F32)
            u = jnp.dot(hb, wfi_ref[:, D_FF + lo:D_FF + hi], preferred_element_type=F32)
            acc = acc + jnp.dot((_silu(a) * u).astype(BF16), wfo_ref[lo:hi, :],
                                preferred_element_type=F32)
            yield
        x2 = x1 + g2_ref[bs] * per_tok(acc)
        y_ref[bs, ts, :] = _rms(x2, nfin_ref[...])
        yield

    _trace_skewed([tile(bs, ts) for bs, ts in _subtiles(bb, tl)], lead=2)


def _post(x, oa, ob, gta, gtb, g1, sc2, sh2, g2, norm_ffn, norm_final,
          wa, wb, wo, wfi, wfo, bb, tl):
    b, l, d = x.shape
    tok = lambda w: pl.BlockSpec((bb, tl, w), lambda i, j: (i, j, 0))
    per_b = pl.BlockSpec((bb, 1, d), lambda i, j: (i, 0, 0))
    return pl.pallas_call(
        _post_kernel,
        out_shape=jax.ShapeDtypeStruct((b, l, d), F32),
        grid=(b // bb, l // tl),
        in_specs=[tok(d), tok(V_A), tok(W_B), tok(d), tok(d), per_b, per_b, per_b, per_b,
                  _resident((1, d)), _resident((1, d)), _resident(wa.shape),
                  _resident(wb.shape), _resident(wo.shape), _resident(wfi.shape),
                  _resident(wfo.shape)],
        out_specs=tok(d),
        compiler_params=_params("arbitrary", "arbitrary"),
        name="merge_out_ffn",
    )(x, oa, ob, gta, gtb, g1, sc2, sh2, g2, norm_ffn, norm_final, wa, wb, wo, wfi, wfo)


def _trunk(x, mod, s0, cache_k, cache_v, bias, w, *, in_tile, post_tile, hgrn_tile, attn_tile):
    b, l, d = x.shape
    sh1, sc1, g1, sh2, sc2, g2 = [m.reshape(b, 1, d) for m in jnp.split(mod, 6, axis=-1)]

    qa, lf, ka, va, gs, qb, kb, vb, gta, gtb, k_rows, v_rows = _inproj(
        x, sc1, sh1, w["norm_mix"], w["lb_logits"], w["w_in"], *in_tile)

    oa, s_fin = _hgrn(qa, lf, ka, va, gs, w["out_norm"], s0, *hgrn_tile)
    ob = _attention(qb, kb, vb, cache_k, cache_v, bias, *attn_tile)

    y = _post(x, oa, ob, gta, gtb, g1, sc2, sh2, g2, w["norm_ffn"], w["norm_final"],
              w["w_branch_a"], w["w_branch_b"], w["w_out"], w["w_ffn_in"], w["w_ffn_out"],
              *post_tile)
    def rows_first(r):
        if r.ndim == 4:
            return r.transpose(0, 3, 1, 2)[None]
        return r.reshape(1, b, r.shape[1] // H_B, H_B, DH_B)

    return y, s_fin[None], rows_first(k_rows), rows_first(v_rows)


def kernel(x_prompt, x_sample, c_prompt, c_sample, state_hgrn, cache_k, cache_v, w_ada, b_ada,
           norm_mix, w_in, hgrn_lb_logits, hgrn_out_norm, w_branch_a, rel_bias, w_branch_b,
           w_out, norm_ffn, w_ffn_in, w_ffn_out, norm_final):
    assert w_ada.shape[0] == 1, "single-layer trunk"
    w = dict(
        w_ada=w_ada[0], b_ada=b_ada[0], norm_mix=norm_mix[0].reshape(1, D_MODEL),
        lb_logits=hgrn_lb_logits, w_in=w_in[0].astype(BF16),
        out_norm=hgrn_out_norm[0].reshape(1, V_A),
        w_branch_a=w_branch_a[0].astype(BF16), w_branch_b=w_branch_b[0].astype(BF16),
        w_out=w_out[0].astype(BF16), norm_ffn=norm_ffn[0].reshape(1, D_MODEL),
        w_ffn_in=w_ffn_in[0].astype(BF16), w_ffn_out=w_ffn_out[0].astype(BF16),
        norm_final=norm_final.reshape(1, D_MODEL),
    )
    bias = _bias_tile(rel_bias[0])

    bp = x_prompt.shape[0]
    bs, ls = x_sample.shape[:2]
    mod = _modulation(jnp.concatenate([c_prompt, c_sample], axis=0), w["w_ada"], w["b_ada"])

    s0_prompt = jnp.zeros((bp, H_A, DK_A, DV_A), F32)
    y_p, s_p, k_p, v_p = _trunk(x_prompt, mod[:bp], s0_prompt, None, None, bias, w,
                                in_tile=(1, 512), post_tile=(1, 512), hgrn_tile=(bp, 512),
                                attn_tile=(1, 2 * WINDOW))
    y_s, s_s, k_s, v_s = _trunk(x_sample, mod[bp:], state_hgrn[0],
                                cache_k[0].transpose(0, 2, 3, 1), cache_v[0].transpose(0, 2, 3, 1),
                                bias, w, in_tile=(8, ls), post_tile=(8, ls), hgrn_tile=(8, ls),
                                attn_tile=(4, ls))
    return (y_p, y_s, s_p, k_p, v_p, s_s, k_s, v_s)
```

```python
import functools

import jax
import jax.numpy as jnp
from jax import lax
from jax.experimental import pallas as pl
from jax.experimental.pallas import tpu as pltpu

D_MODEL = 1024
CHUNK = 64
H_A, DK_A, DV_A = 4, 128, 128
K_A = H_A * DK_A
V_A = H_A * DV_A
H_B, DH_B = 8, 64
W_B = H_B * DH_B
N_PAST_CHUNKS = 8
WINDOW = N_PAST_CHUNKS * CHUNK
BAND = WINDOW + CHUNK
REL_CLIP = 128
NUM_REL = CHUNK + REL_CLIP
D_FF = 2816
IN_COLS = 4 * K_A + 3 * W_B + 2 * D_MODEL
EPS = 1e-6
NEG = -1e30
_LOG2E = 1.4426950408889634
_Q_SCALE = (DH_B ** -0.5) * _LOG2E

F32 = jnp.float32
BF16 = jnp.bfloat16

_OFF_QA, _OFF_FA, _OFF_IA, _OFF_GA = 0, K_A, 2 * K_A, 2 * K_A + V_A
_OFF_QB = 2 * K_A + 2 * V_A
_OFF_KB = _OFF_QB + W_B
_OFF_VB = _OFF_KB + W_B
_OFF_GATE_A = _OFF_VB + W_B
_OFF_GATE_B = _OFF_GATE_A + D_MODEL

_VMEM_LIMIT_BYTES = 56 * 1024 * 1024
_LANES = 128
_MXU_DIM = 256
_FFN_CHUNKS = ((0, 6 * _MXU_DIM), (6 * _MXU_DIM, D_FF))
assert D_FF % _MXU_DIM == 0


def _params(*sem):
    return pltpu.CompilerParams(dimension_semantics=sem, vmem_limit_bytes=_VMEM_LIMIT_BYTES)


def _resident(shape):
    zeros = (0,) * len(shape)
    return pl.BlockSpec(shape, lambda *_: zeros, pipeline_mode=pl.Buffered(1))


def _sigmoid(x):
    return 0.5 * jnp.tanh(0.5 * x) + 0.5


def _silu(x):
    return x * _sigmoid(x)


def _rms(x, g):
    return x * lax.rsqrt(jnp.mean(x * x, axis=-1, keepdims=True) + EPS) * g


def _split3(x):
    hi = x.astype(BF16)
    r = x - hi.astype(F32)
    mid = r.astype(BF16)
    lo = (r - mid.astype(F32)).astype(BF16)
    return hi, mid, lo


def _mod_kernel(c_ref, w_ref, b_ref, o_ref):
    s_hi, s_lo, _ = _split3(_silu(c_ref[...]))
    w_hi, w_lo, _ = _split3(w_ref[...])
    dot = lambda a, b: jnp.dot(a, b, preferred_element_type=F32)
    o_ref[...] = dot(s_hi, w_hi) + (dot(s_lo, w_hi) + dot(s_hi, w_lo)) + b_ref[...]


def _modulation(c, w_ada, b_ada):
    b = c.shape[0]
    n_out = w_ada.shape[1]
    return pl.pallas_call(
        _mod_kernel,
        out_shape=jax.ShapeDtypeStruct((b, n_out), F32),
        grid=(n_out // D_MODEL,),
        in_specs=[
            pl.BlockSpec((b, D_MODEL), lambda j: (0, 0)),
            pl.BlockSpec((D_MODEL, D_MODEL), lambda j: (0, j)),
            pl.BlockSpec((1, D_MODEL), lambda j: (0, j)),
        ],
        out_specs=pl.BlockSpec((b, D_MODEL), lambda j: (0, j)),
        compiler_params=_params("arbitrary"),
        name="adaln_mod",
    )(c, w_ada, b_ada.reshape(1, n_out))


_BIAS_EXT = 640


def _bias_kernel(rb_ref, o_ref):
    m = lax.broadcasted_iota(jnp.int32, (NUM_REL, _BIAS_EXT), 1)
    slot = lax.broadcasted_iota(jnp.int32, (NUM_REL, _BIAS_EXT), 0)
    idx = jnp.clip(WINDOW + CHUNK - 1 - m, -(CHUNK - 1), REL_CLIP) + (CHUNK - 1)
    onehot = jnp.where(slot == idx, 1.0, 0.0).astype(BF16)
    ext = sum(jnp.dot(piece, onehot, preferred_element_type=F32)
              for piece in _split3(rb_ref[...]))
    ext = ext * _LOG2E
    for i in range(CHUNK):
        o_ref[i] = ext[:, CHUNK - 1 - i:CHUNK - 1 - i + BAND]


def _bias_tile(rel_bias):
    out = pl.pallas_call(
        _bias_kernel,
        out_shape=jax.ShapeDtypeStruct((CHUNK, H_B, BAND), F32),
        grid=(1,),
        in_specs=[pl.BlockSpec((H_B, NUM_REL), lambda i: (0, 0))],
        out_specs=pl.BlockSpec((CHUNK, H_B, BAND), lambda i: (0, 0, 0)),
        compiler_params=_params("arbitrary"),
        name="rel_bias_tile",
    )(rel_bias)
    return out.transpose(1, 0, 2)


_SUB_ROWS = 256


def _subtiles(bb, tl):
    if bb == 1:
        step = min(tl, _SUB_ROWS)
        return [(slice(0, 1), slice(t, t + step)) for t in range(0, tl, step)]
    step = max(1, min(bb, _SUB_ROWS // tl))
    return [(slice(i, i + step), slice(0, tl)) for i in range(0, bb, step)]


def _trace_skewed(tiles, lead):
    tiles = list(tiles)
    started, live = 0, []
    while started < len(tiles) or live:
        if started < len(tiles) and (not live or live[-1][1] >= lead):
            live.append([tiles[started], 0])
            started += 1
        for entry in list(live):
            try:
                next(entry[0])
                entry[1] += 1
            except StopIteration:
                live.remove(entry)


def _inproj_kernel(x_ref, sc_ref, sh_ref, g_ref, lbl_ref, w_ref,
                   qa_ref, lf_ref, ka_ref, va_ref, gs_ref, qb_ref, kb_ref, vb_ref,
                   gta_ref, gtb_ref, kf_ref, vf_ref, *, n_keep, rows_minor):
    bb, tl, d = x_ref.shape
    j = pl.program_id(1)
    keep = j >= pl.num_programs(1) - n_keep

    lbl = lbl_ref[...]
    e = jnp.exp(lbl - jnp.max(lbl, axis=0, keepdims=True))
    lb = e[0:1] / jnp.sum(e, axis=0, keepdims=True)

    def tile(bs, ts):
        x = x_ref[bs, ts, :]
        h = _rms(x, g_ref[...]) * (1.0 + sc_ref[bs]) + sh_ref[bs]
        hb = h.reshape(-1, d).astype(BF16)
        yield

        def proj(off, width):
            return jnp.dot(hb, w_ref[:, off:off + width], preferred_element_type=F32)

        def put(ref, val):
            ref[bs, ts, :] = val.reshape(x.shape[0], x.shape[1], -1).astype(ref.dtype)

        put(gta_ref, _sigmoid(proj(_OFF_GATE_A, D_MODEL)))
        yield
        put(gtb_ref, _sigmoid(proj(_OFF_GATE_B, D_MODEL)))
        yield
        put(qa_ref, _silu(proj(_OFF_QA, K_A)))
        f = lb + (1.0 - lb) * _sigmoid(proj(_OFF_FA, K_A))
        put(lf_ref, jnp.log(f))
        put(ka_ref, 1.0 - f)
        yield
        put(gs_ref, _silu(proj(_OFF_GA, V_A)))
        put(va_ref, proj(_OFF_IA, V_A))
        yield
        put(qb_ref, proj(_OFF_QB, W_B) * _Q_SCALE)
        kb = proj(_OFF_KB, W_B)
        vb = proj(_OFF_VB, W_B)
        put(kb_ref, kb)
        put(vb_ref, vb)
        cache_rows.append((bs, ts, kb, vb))
        yield

    cache_rows = []
    _trace_skewed([tile(bs, ts) for bs, ts in _subtiles(bb, tl)], lead=2)

    @pl.when(keep)
    def _():
        for bs, ts, kb, vb in cache_rows:
            nb, nt_ = bs.stop - bs.start, ts.stop - ts.start
            for ref, val in ((kf_ref, kb), (vf_ref, vb)):
                if rows_minor:
                    assert nb == 1
                    ref[bs, :, :, ts] = val.T.reshape(1, H_B, DH_B, nt_)
                else:
                    for hd in range(H_B):
                        ref[bs, pl.ds(ts.start * H_B + hd, nt_, stride=H_B), :] = (
                            val[:, hd * DH_B:(hd + 1) * DH_B].reshape(nb, nt_, DH_B))


def _inproj(x, sc, sh, norm_g, lb_logits, w_in_bf16, bb, tl):
    b, l, d = x.shape
    nj = l // tl
    keep_rows = min(WINDOW, l)
    assert keep_rows % tl == 0
    n_keep = keep_rows // tl
    widths = [K_A, K_A, K_A, V_A, V_A, W_B, W_B, W_B, D_MODEL, D_MODEL]
    dtypes = [BF16, F32, BF16, BF16, BF16, BF16, BF16, BF16, BF16, BF16]
    tok = lambda w: pl.BlockSpec((bb, tl, w), lambda i, j: (i, j, 0))
    per_b = pl.BlockSpec((bb, 1, d), lambda i, j: (i, 0, 0))
    kept_block = lambda j: jnp.maximum(j - (nj - n_keep), 0)
    rows_minor = keep_rows >= _LANES
    if rows_minor:
        kept = pl.BlockSpec((bb, H_B, DH_B, tl), lambda i, j: (i, 0, 0, kept_block(j)))
        kept_shape = jax.ShapeDtypeStruct((b, H_B, DH_B, keep_rows), F32)
    else:
        kept = pl.BlockSpec((bb, tl * H_B, DH_B), lambda i, j: (i, kept_block(j), 0))
        kept_shape = jax.ShapeDtypeStruct((b, keep_rows * H_B, DH_B), F32)
    return pl.pallas_call(
        functools.partial(_inproj_kernel, n_keep=n_keep, rows_minor=rows_minor),
        out_shape=([jax.ShapeDtypeStruct((b, l, w), dt) for w, dt in zip(widths, dtypes)]
                   + [kept_shape] * 2),
        grid=(b // bb, nj),
        in_specs=[tok(d), per_b, per_b, _resident((1, d)), _resident(lb_logits.shape),
                  _resident(w_in_bf16.shape)],
        out_specs=[tok(w) for w in widths] + [kept, kept],
        compiler_params=_params("arbitrary", "arbitrary"),
        name="norm_inproj",
    )(x, sc, sh, norm_g, lb_logits, w_in_bf16)


_HGRN_TILE = 256


def _hgrn_kernel(qa_ref, lf_ref, ka_ref, va_ref, gs_ref, og_ref, s0_ref,
                 o_ref, sfin_ref, st_ref):
    j = pl.program_id(1)
    bb, tb, _ = qa_ref.shape
    th = min(tb, _HGRN_TILE)
    nc = th // CHUNK

    @pl.when(j == 0)
    def _():
        for bi in range(bb):
            for h in range(H_A):
                st_ref[bi, h] = s0_ref[bi, h].T

    row = lax.broadcasted_iota(jnp.int32, (th, th), 0)
    col = lax.broadcasted_iota(jnp.int32, (th, th), 1)
    tri = (row >= col) & (row // CHUNK == col // CHUNK)
    tri_bf = jnp.where(tri, 1.0, 0.0).astype(BF16)
    nt = (((1,), (1,)), ((), ()))
    tn = (((0,), (0,)), ((), ()))
    chunked = lambda a: a.reshape(nc, CHUNK, a.shape[-1])
    flat = lambda a: a.reshape(th, a.shape[-1])

    def seq_tile(t0, bi):
        tile = slice(t0, t0 + th)
        cum = chunked(sum(jnp.dot(tri_bf, piece, preferred_element_type=F32)
                          for piece in _split3(lf_ref[bi, tile, :])[:2]))
        mid = cum[:, CHUNK // 2:CHUNK // 2 + 1]
        tot = cum[:, CHUNK - 1:CHUNK]
        q = chunked(qa_ref[bi, tile, :].astype(F32))
        k = chunked(ka_ref[bi, tile, :].astype(F32))
        v = va_ref[bi, tile, :].astype(BF16)
        qe32 = q * jnp.exp(cum - mid)
        ke32 = k * jnp.exp(mid - cum)
        qe = flat(qe32).astype(BF16)
        ke = flat(ke32).astype(BF16)
        qd = flat(qe32 * jnp.exp(mid)).astype(BF16)
        kd = flat(ke32 * jnp.exp(tot - mid)).astype(BF16)
        decay = jnp.exp(tot)
        yield
        for h in range(H_A):
            lanes = slice(h * DK_A, (h + 1) * DK_A)
            a = lax.dot_general(qe[:, lanes], ke[:, lanes], nt, preferred_element_type=F32)
            a = jnp.where(tri, a, 0.0).astype(BF16)
            intra = jnp.dot(a, v[:, lanes], preferred_element_type=F32)
            st = st_ref[bi, h]
            inter = []
            for c in range(nc):
                rows = slice(c * CHUNK, (c + 1) * CHUNK)
                inter.append(lax.dot_general(qd[rows, lanes], st.astype(BF16), nt,
                                             preferred_element_type=F32))
                st = decay[c, :, lanes] * st + lax.dot_general(
                    v[rows, lanes], kd[rows, lanes], tn, preferred_element_type=F32)
            st_ref[bi, h] = st
            o = intra + jnp.concatenate(inter, axis=0)
            o = o * lax.rsqrt(jnp.mean(o * o, axis=-1, keepdims=True) + EPS)
            gated = o * og_ref[:, lanes] * gs_ref[bi, tile, lanes].astype(F32)
            o_ref[bi, tile, lanes] = gated.astype(o_ref.dtype)
            yield

    _trace_skewed([seq_tile(t0, bi) for t0 in range(0, tb, th) for bi in range(bb)], lead=2)

    @pl.when(j == pl.num_programs(1) - 1)
    def _():
        for bi in range(bb):
            for h in range(H_A):
                sfin_ref[bi, h] = st_ref[bi, h].T


def _hgrn(qa, lf, ka, va, gs, out_norm, s0, bb, th):
    b, l, _ = qa.shape
    tok = pl.BlockSpec((bb, th, K_A), lambda i, j: (i, j, 0))
    state = pl.BlockSpec((bb, H_A, DK_A, DV_A), lambda i, j: (i, 0, 0, 0))
    return pl.pallas_call(
        _hgrn_kernel,
        out_shape=[jax.ShapeDtypeStruct((b, l, V_A), BF16),
                   jax.ShapeDtypeStruct((b, H_A, DK_A, DV_A), F32)],
        grid=(b // bb, l // th),
        in_specs=[tok, tok, tok, tok, tok, _resident((1, V_A)), state],
        out_specs=[tok, state],
        scratch_shapes=[pltpu.VMEM((bb, H_A, DV_A, DK_A), F32)],
        compiler_params=_params("arbitrary", "arbitrary"),
        name="hgrn2_recurrence",
    )(qa, lf, ka, va, gs, out_norm, s0)


_HEADS_PER_UNIT = 4
_UNIT_LANES = _HEADS_PER_UNIT * DH_B
_UNIT_ROWS = _HEADS_PER_UNIT * CHUNK


def _attn_kernel(q_ref, kp_ref, kc_ref, vp_ref, vc_ref, bias_ref, o_ref, kbuf, vbuf, *,
                 mask_first_tile):
    t = pl.program_id(1)
    bb, tq, _ = q_ref.shape
    for prev, cur, buf in ((kp_ref, kc_ref, kbuf), (vp_ref, vc_ref, vbuf)):
        buf[:, 0:WINDOW, :] = prev[...].astype(BF16)
        buf[:, WINDOW:WINDOW + tq, :] = cur[...].astype(BF16)

    lane_head = lax.broadcasted_iota(jnp.int32, (CHUNK, _UNIT_LANES), 1) // DH_B
    own = [lane_head == h for h in range(_HEADS_PER_UNIT)]
    key = lax.broadcasted_iota(jnp.int32, (_UNIT_ROWS, BAND), 1)
    nt = (((1,), (1,)), ((), ()))

    def tile(masked):
        for bi in range(bb):
            for c in range(tq // CHUNK):
                rows = slice(c * CHUNK, (c + 1) * CHUNK)
                band = slice(c * CHUNK, c * CHUNK + BAND)
                for g in range(H_B // _HEADS_PER_UNIT):
                    lanes = slice(g * _UNIT_LANES, (g + 1) * _UNIT_LANES)
                    q4 = q_ref[bi, rows, lanes]
                    qs = jnp.concatenate([jnp.where(m, q4, jnp.zeros_like(q4)) for m in own],
                                         axis=0).astype(BF16)
                    s = lax.dot_general(qs, kbuf[bi, band, lanes], nt,
                                        preferred_element_type=F32)
                    s = s + bias_ref[g]
                    if masked:
                        s = jnp.where(key + c * CHUNK >= WINDOW, s, NEG)
                    p = jnp.exp2(s - jnp.max(s, axis=-1, keepdims=True))
                    inv = 1.0 / jnp.sum(p, axis=-1, keepdims=True)
                    o4 = jnp.dot(p.astype(BF16), vbuf[bi, band, lanes],
                                 preferred_element_type=F32) * inv
                    o = jnp.where(own[0], o4[0:CHUNK], 0.0)
                    for h in range(1, _HEADS_PER_UNIT):
                        o = jnp.where(own[h], o4[h * CHUNK:(h + 1) * CHUNK], o)
                    o_ref[bi, rows, lanes] = o.astype(o_ref.dtype)

    if mask_first_tile:
        pl.when(t == 0)(lambda: tile(True))
        pl.when(t > 0)(lambda: tile(False))
    else:
        tile(False)


def _attn_cache_kernel(q_ref, kt_ref, kc_ref, vt_ref, vc_ref, bias_ref, o_ref):
    bb = q_ref.shape[0]
    lane_head = lax.broadcasted_iota(jnp.int32, (CHUNK, _UNIT_LANES), 1) // DH_B
    own = [lane_head == h for h in range(_HEADS_PER_UNIT)]
    nt = (((1,), (1,)), ((), ()))
    dot = lambda a, b: jnp.dot(a, b, preferred_element_type=F32)
    dot_nt = lambda a, b: lax.dot_general(a, b, nt, preferred_element_type=F32)

    for bi in range(bb):
        for g in range(H_B // _HEADS_PER_UNIT):
            lanes = slice(g * _UNIT_LANES, (g + 1) * _UNIT_LANES)
            heads = slice(g * _HEADS_PER_UNIT, (g + 1) * _HEADS_PER_UNIT)
            kt4 = kt_ref[bi, heads].reshape(_UNIT_LANES, WINDOW).astype(BF16)
            vt4 = vt_ref[bi, heads].reshape(_UNIT_LANES, WINDOW).astype(BF16)
            q4 = q_ref[bi, :, lanes]
            qs = jnp.concatenate([jnp.where(m, q4, jnp.zeros_like(q4)) for m in own],
                                 axis=0).astype(BF16)
            s_old = dot(qs, kt4) + bias_ref[g, :, 0:WINDOW]
            s_new = dot_nt(qs, kc_ref[bi, :, lanes]) + bias_ref[g, :, WINDOW:BAND]
            top = jnp.maximum(jnp.max(s_old, axis=-1, keepdims=True),
                              jnp.max(s_new, axis=-1, keepdims=True))
            p_old = jnp.exp2(s_old - top)
            p_new = jnp.exp2(s_new - top)
            inv = 1.0 / (jnp.sum(p_old, axis=-1, keepdims=True)
                         + jnp.sum(p_new, axis=-1, keepdims=True))
            o4 = (dot_nt(p_old.astype(BF16), vt4)
                  + dot(p_new.astype(BF16), vc_ref[bi, :, lanes])) * inv
            o = jnp.where(own[0], o4[0:CHUNK], 0.0)
            for h in range(1, _HEADS_PER_UNIT):
                o = jnp.where(own[h], o4[h * CHUNK:(h + 1) * CHUNK], o)
            o_ref[bi, :, lanes] = o.astype(o_ref.dtype)


def _attention(q, k, v, cache_kt, cache_vt, bias, bb, tq):
    b, l, _ = q.shape
    cur = pl.BlockSpec((bb, tq, W_B), lambda i, j: (i, j, 0))
    bias = bias.reshape(H_B // _HEADS_PER_UNIT, _UNIT_ROWS, BAND)
    out_shape = jax.ShapeDtypeStruct((b, l, W_B), BF16)
    if cache_kt is None:
        assert tq % WINDOW == 0
        per_tile = tq // WINDOW
        prev = pl.BlockSpec((bb, WINDOW, W_B),
                            lambda i, j: (i, jnp.maximum(j * per_tile - 1, 0), 0))
        return pl.pallas_call(
            functools.partial(_attn_kernel, mask_first_tile=True),
            out_shape=out_shape,
            grid=(b // bb, l // tq),
            in_specs=[cur, prev, cur, prev, cur, _resident(bias.shape)],
            out_specs=cur,
            scratch_shapes=[pltpu.VMEM((bb, WINDOW + tq, W_B), BF16),
                            pltpu.VMEM((bb, WINDOW + tq, W_B), BF16)],
            compiler_params=_params("arbitrary", "arbitrary"),
            name="band_attention",
        )(q, k, k, v, v, bias)
    assert tq == l == CHUNK and cache_kt.shape[1:] == (H_B, DH_B, WINDOW)
    cached = pl.BlockSpec((bb, H_B, DH_B, WINDOW), lambda i, j: (i, 0, 0, 0))
    return pl.pallas_call(
        _attn_cache_kernel,
        out_shape=out_shape,
        grid=(b // bb, 1),
        in_specs=[cur, cached, cur, cached, cur, _resident(bias.shape)],
        out_specs=cur,
        compiler_params=_params("arbitrary", "arbitrary"),
        name="cache_attention",
    )(q, cache_kt, k, cache_vt, v, bias)


def _post_kernel(x_ref, oa_ref, ob_ref, gta_ref, gtb_ref, g1_ref, sc2_ref, sh2_ref, g2_ref,
                 nffn_ref, nfin_ref, wa_ref, wb_ref, wo_ref, wfi_ref, wfo_ref, y_ref):
    bb, tl, d = x_ref.shape

    def tile(bs, ts):
        x = x_ref[bs, ts, :]
        n = x.shape[0] * x.shape[1]
        flat = lambda ref: ref[bs, ts, :].reshape(n, ref.shape[-1])
        per_tok = lambda v: v.reshape(x.shape)

        ya = jnp.dot(flat(oa_ref), wa_ref[...], preferred_element_type=F32)
        yb = jnp.dot(flat(ob_ref), wb_ref[...], preferred_element_type=F32)
        merged = flat(gta_ref).astype(F32) * ya + flat(gtb_ref).astype(F32) * yb
        yield
        y = jnp.dot(merged.astype(BF16), wo_ref[...], preferred_element_type=F32)
        x1 = x + g1_ref[bs] * per_tok(y)
        h2 = _rms(x1, nffn_ref[...]) * (1.0 + sc2_ref[bs]) + sh2_ref[bs]
        hb = h2.reshape(n, d).astype(BF16)
        yield
        acc = jnp.zeros((n, d), F32)
        for lo, hi in _FFN_CHUNKS:
            a = jnp.dot(hb, wfi_ref[:, lo:hi], preferred_element_type=F32)
            u = jnp.dot(hb, wfi_ref[:, D_FF + lo:D_FF + hi], preferred_element_type=F32)
            acc = acc + jnp.dot((_silu(a) * u).astype(BF16), wfo_ref[lo:hi, :],
                                preferred_element_type=F32)
            yield
        x2 = x1 + g2_ref[bs] * per_tok(acc)
        y_ref[bs, ts, :] = _rms(x2, nfin_ref[...])
        yield

    _trace_skewed([tile(bs, ts) for bs, ts in _subtiles(bb, tl)], lead=2)


def _post(x, oa, ob, gta, gtb, g1, sc2, sh2, g2, norm_ffn, norm_final,
          wa, wb, wo, wfi, wfo, bb, tl):
    b, l, d = x.shape
    tok = lambda w: pl.BlockSpec((bb, tl, w), lambda i, j: (i, j, 0))
    per_b = pl.BlockSpec((bb, 1, d), lambda i, j: (i, 0, 0))
    return pl.pallas_call(
        _post_kernel,
        out_shape=jax.ShapeDtypeStruct((b, l, d), F32),
        grid=(b // bb, l // tl),
        in_specs=[tok(d), tok(V_A), tok(W_B), tok(d), tok(d), per_b, per_b, per_b, per_b,
                  _resident((1, d)), _resident((1, d)), _resident(wa.shape),
                  _resident(wb.shape), _resident(wo.shape), _resident(wfi.shape),
                  _resident(wfo.shape)],
        out_specs=tok(d),
        compiler_params=_params("arbitrary", "arbitrary"),
        name="merge_out_ffn",
    )(x, oa, ob, gta, gtb, g1, sc2, sh2, g2, norm_ffn, norm_final, wa, wb, wo, wfi, wfo)


def _trunk(x, mod, s0, cache_k, cache_v, bias, w, *, in_tile, post_tile, hgrn_tile, attn_tile):
    b, l, d = x.shape
    sh1, sc1, g1, sh2, sc2, g2 = [m.reshape(b, 1, d) for m in jnp.split(mod, 6, axis=-1)]

    qa, lf, ka, va, gs, qb, kb, vb, gta, gtb, k_rows, v_rows = _inproj(
        x, sc1, sh1, w["norm_mix"], w["lb_logits"], w["w_in"], *in_tile)

    oa, s_fin = _hgrn(qa, lf, ka, va, gs, w["out_norm"], s0, *hgrn_tile)
    ob = _attention(qb, kb, vb, cache_k, cache_v, bias, *attn_tile)

    y = _post(x, oa, ob, gta, gtb, g1, sc2, sh2, g2, w["norm_ffn"], w["norm_final"],
              w["w_branch_a"], w["w_branch_b"], w["w_out"], w["w_ffn_in"], w["w_ffn_out"],
              *post_tile)
    def rows_first(r):
        if r.ndim == 4:
            return r.transpose(0, 3, 1, 2)[None]
        return r.reshape(1, b, r.shape[1] // H_B, H_B, DH_B)

    return y, s_fin[None], rows_first(k_rows), rows_first(v_rows)


def kernel(x_prompt, x_sample, c_prompt, c_sample, state_hgrn, cache_k, cache_v, w_ada, b_ada,
           norm_mix, w_in, hgrn_lb_logits, hgrn_out_norm, w_branch_a, rel_bias, w_branch_b,
           w_out, norm_ffn, w_ffn_in, w_ffn_out, norm_final):
    assert w_ada.shape[0] == 1, "single-layer trunk"
    w = dict(
        w_ada=w_ada[0], b_ada=b_ada[0], norm_mix=norm_mix[0].reshape(1, D_MODEL),
        lb_logits=hgrn_lb_logits, w_in=w_in[0].astype(BF16),
        out_norm=hgrn_out_norm[0].reshape(1, V_A),
        w_branch_a=w_branch_a[0].astype(BF16), w_branch_b=w_branch_b[0].astype(BF16),
        w_out=w_out[0].astype(BF16), norm_ffn=norm_ffn[0].reshape(1, D_MODEL),
        w_ffn_in=w_ffn_in[0].astype(BF16), w_ffn_out=w_ffn_out[0].astype(BF16),
        norm_final=norm_final.reshape(1, D_MODEL),
    )
    bias = _bias_tile(rel_bias[0])

    bp = x_prompt.shape[0]
    bs, ls = x_sample.shape[:2]
    mod = _modulation(jnp.concatenate([c_prompt, c_sample], axis=0), w["w_ada"], w["b_ada"])

    s0_prompt = jnp.zeros((bp, H_A, DK_A, DV_A), F32)
    y_p, s_p, k_p, v_p = _trunk(x_prompt, mod[:bp], s0_prompt, None, None, bias, w,
                                in_tile=(1, 512), post_tile=(1, 512), hgrn_tile=(bp, 1024),
                                attn_tile=(1, 2 * WINDOW))
    y_s, s_s, k_s, v_s = _trunk(x_sample, mod[bp:], state_hgrn[0],
                                cache_k[0].transpose(0, 2, 3, 1), cache_v[0].transpose(0, 2, 3, 1),
                                bias, w, in_tile=(8, ls), post_tile=(8, ls), hgrn_tile=(8, ls),
                                attn_tile=(4, ls))
    return (y_p, y_s, s_p, k_p, v_p, s_s, k_s, v_s)
```

```python
import functools

import jax
import jax.numpy as jnp
from jax import lax
from jax.experimental import pallas as pl
from jax.experimental.pallas import tpu as pltpu

D_MODEL = 1024
CHUNK = 64
H_A, DK_A, DV_A = 4, 128, 128
K_A = H_A * DK_A
V_A = H_A * DV_A
H_B, DH_B = 8, 64
W_B = H_B * DH_B
N_PAST_CHUNKS = 8
WINDOW = N_PAST_CHUNKS * CHUNK
BAND = WINDOW + CHUNK
REL_CLIP = 128
NUM_REL = CHUNK + REL_CLIP
D_FF = 2816
IN_COLS = 4 * K_A + 3 * W_B + 2 * D_MODEL
EPS = 1e-6
NEG = -1e30
_LOG2E = 1.4426950408889634
_Q_SCALE = (DH_B ** -0.5) * _LOG2E

F32 = jnp.float32
BF16 = jnp.bfloat16

_OFF_QA, _OFF_FA, _OFF_IA, _OFF_GA = 0, K_A, 2 * K_A, 2 * K_A + V_A
_OFF_QB = 2 * K_A + 2 * V_A
_OFF_KB = _OFF_QB + W_B
_OFF_VB = _OFF_KB + W_B
_OFF_GATE_A = _OFF_VB + W_B
_OFF_GATE_B = _OFF_GATE_A + D_MODEL

_VMEM_LIMIT_BYTES = 56 * 1024 * 1024
_POST_VMEM_LIMIT_BYTES = 60 * 1024 * 1024
_LANES = 128
_MXU_DIM = 256
_FFN_CHUNKS = ((0, 6 * _MXU_DIM), (6 * _MXU_DIM, D_FF))
assert D_FF % _MXU_DIM == 0


def _params(*sem):
    return pltpu.CompilerParams(dimension_semantics=sem, vmem_limit_bytes=_VMEM_LIMIT_BYTES)


def _resident(shape):
    zeros = (0,) * len(shape)
    return pl.BlockSpec(shape, lambda *_: zeros, pipeline_mode=pl.Buffered(1))


def _sigmoid(x):
    return 0.5 * jnp.tanh(0.5 * x) + 0.5


def _silu(x):
    return x * _sigmoid(x)


def _rms(x, g):
    return x * lax.rsqrt(jnp.mean(x * x, axis=-1, keepdims=True) + EPS) * g


def _split3(x):
    hi = x.astype(BF16)
    r = x - hi.astype(F32)
    mid = r.astype(BF16)
    lo = (r - mid.astype(F32)).astype(BF16)
    return hi, mid, lo


def _mod_kernel(c_ref, w_ref, b_ref, o_ref):
    s_hi, s_lo, _ = _split3(_silu(c_ref[...]))
    w_hi, w_lo, _ = _split3(w_ref[...])
    dot = lambda a, b: jnp.dot(a, b, preferred_element_type=F32)
    o_ref[...] = dot(s_hi, w_hi) + (dot(s_lo, w_hi) + dot(s_hi, w_lo)) + b_ref[...]


def _modulation(c, w_ada, b_ada):
    b = c.shape[0]
    n_out = w_ada.shape[1]
    return pl.pallas_call(
        _mod_kernel,
        out_shape=jax.ShapeDtypeStruct((b, n_out), F32),
        grid=(n_out // D_MODEL,),
        in_specs=[
            pl.BlockSpec((b, D_MODEL), lambda j: (0, 0)),
            pl.BlockSpec((D_MODEL, D_MODEL), lambda j: (0, j)),
            pl.BlockSpec((1, D_MODEL), lambda j: (0, j)),
        ],
        out_specs=pl.BlockSpec((b, D_MODEL), lambda j: (0, j)),
        compiler_params=_params("arbitrary"),
        name="adaln_mod",
    )(c, w_ada, b_ada.reshape(1, n_out))


_BIAS_EXT = 640


def _bias_kernel(rb_ref, o_ref):
    m = lax.broadcasted_iota(jnp.int32, (NUM_REL, _BIAS_EXT), 1)
    slot = lax.broadcasted_iota(jnp.int32, (NUM_REL, _BIAS_EXT), 0)
    idx = jnp.clip(WINDOW + CHUNK - 1 - m, -(CHUNK - 1), REL_CLIP) + (CHUNK - 1)
    onehot = jnp.where(slot == idx, 1.0, 0.0).astype(BF16)
    ext = sum(jnp.dot(piece, onehot, preferred_element_type=F32)
              for piece in _split3(rb_ref[...]))
    ext = ext * _LOG2E
    for i in range(CHUNK):
        o_ref[i] = ext[:, CHUNK - 1 - i:CHUNK - 1 - i + BAND]


def _bias_tile(rel_bias):
    out = pl.pallas_call(
        _bias_kernel,
        out_shape=jax.ShapeDtypeStruct((CHUNK, H_B, BAND), F32),
        grid=(1,),
        in_specs=[pl.BlockSpec((H_B, NUM_REL), lambda i: (0, 0))],
        out_specs=pl.BlockSpec((CHUNK, H_B, BAND), lambda i: (0, 0, 0)),
        compiler_params=_params("arbitrary"),
        name="rel_bias_tile",
    )(rel_bias)
    return out.transpose(1, 0, 2)


_SUB_ROWS = 256


def _subtiles(bb, tl):
    if bb == 1:
        step = min(tl, _SUB_ROWS)
        return [(slice(0, 1), slice(t, t + step)) for t in range(0, tl, step)]
    step = max(1, min(bb, _SUB_ROWS // tl))
    return [(slice(i, i + step), slice(0, tl)) for i in range(0, bb, step)]


_SH1, _SC1, _G1, _SH2, _SC2, _G2 = range(6)


def _mod_spec(which, bb, row0):
    assert row0 % bb == 0
    return pl.BlockSpec((bb, 1, D_MODEL), lambda i, j: (i + row0 // bb, 0, which))


def _trace_skewed(tiles, lead):
    tiles = list(tiles)
    started, live = 0, []
    while started < len(tiles) or live:
        if started < len(tiles) and (not live or live[-1][1] >= lead):
            live.append([tiles[started], 0])
            started += 1
        for entry in list(live):
            try:
                next(entry[0])
                entry[1] += 1
            except StopIteration:
                live.remove(entry)


def _inproj_kernel(x_ref, sc_ref, sh_ref, g_ref, lbl_ref, w_ref,
                   qa_ref, lf_ref, ka_ref, va_ref, gs_ref, qb_ref, kb_ref, vb_ref,
                   gta_ref, gtb_ref, kf_ref, vf_ref, *, n_keep, rows_minor):
    bb, tl, d = x_ref.shape
    j = pl.program_id(1)
    keep = j >= pl.num_programs(1) - n_keep

    lbl = lbl_ref[...]
    e = jnp.exp(lbl - jnp.max(lbl, axis=0, keepdims=True))
    lb = e[0:1] / jnp.sum(e, axis=0, keepdims=True)

    def tile(bs, ts):
        x = x_ref[bs, ts, :]
        h = _rms(x, g_ref[...]) * (1.0 + sc_ref[bs]) + sh_ref[bs]
        hb = h.reshape(-1, d).astype(BF16)
        yield

        def proj(off, width):
            return jnp.dot(hb, w_ref[:, off:off + width], preferred_element_type=F32)

        def put(ref, val):
            ref[bs, ts, :] = val.reshape(x.shape[0], x.shape[1], -1).astype(ref.dtype)

        put(gta_ref, _sigmoid(proj(_OFF_GATE_A, D_MODEL)))
        yield
        put(gtb_ref, _sigmoid(proj(_OFF_GATE_B, D_MODEL)))
        yield
        put(qa_ref, _silu(proj(_OFF_QA, K_A)))
        f = lb + (1.0 - lb) * _sigmoid(proj(_OFF_FA, K_A))
        put(lf_ref, jnp.log(f))
        put(ka_ref, 1.0 - f)
        yield
        put(gs_ref, _silu(proj(_OFF_GA, V_A)))
        put(va_ref, proj(_OFF_IA, V_A))
        yield
        put(qb_ref, proj(_OFF_QB, W_B) * _Q_SCALE)
        kb = proj(_OFF_KB, W_B)
        vb = proj(_OFF_VB, W_B)
        put(kb_ref, kb)
        put(vb_ref, vb)
        cache_rows.append((bs, ts, kb, vb))
        yield

    cache_rows = []
    _trace_skewed([tile(bs, ts) for bs, ts in _subtiles(bb, tl)], lead=2)

    @pl.when(keep)
    def _():
        for bs, ts, kb, vb in cache_rows:
            nb, nt_ = bs.stop - bs.start, ts.stop - ts.start
            for ref, val in ((kf_ref, kb), (vf_ref, vb)):
                if rows_minor:
                    assert nb == 1
                    ref[bs, :, :, ts] = val.T.reshape(1, H_B, DH_B, nt_)
                else:
                    for hd in range(H_B):
                        ref[bs, pl.ds(ts.start * H_B + hd, nt_, stride=H_B), :] = (
                            val[:, hd * DH_B:(hd + 1) * DH_B].reshape(nb, nt_, DH_B))


def _inproj(x, mod, mod_row0, norm_g, lb_logits, w_in_bf16, bb, tl):
    b, l, d = x.shape
    nj = l // tl
    keep_rows = min(WINDOW, l)
    assert keep_rows % tl == 0
    n_keep = keep_rows // tl
    widths = [K_A, K_A, K_A, V_A, V_A, W_B, W_B, W_B, D_MODEL, D_MODEL]
    dtypes = [BF16, F32, BF16, BF16, BF16, BF16, BF16, BF16, BF16, BF16]
    tok = lambda w: pl.BlockSpec((bb, tl, w), lambda i, j: (i, j, 0))
    kept_block = lambda j: jnp.maximum(j - (nj - n_keep), 0)
    rows_minor = keep_rows >= _LANES
    if rows_minor:
        kept = pl.BlockSpec((bb, H_B, DH_B, tl), lambda i, j: (i, 0, 0, kept_block(j)))
        kept_shape = jax.ShapeDtypeStruct((b, H_B, DH_B, keep_rows), F32)
    else:
        kept = pl.BlockSpec((bb, tl * H_B, DH_B), lambda i, j: (i, kept_block(j), 0))
        kept_shape = jax.ShapeDtypeStruct((b, keep_rows * H_B, DH_B), F32)
    return pl.pallas_call(
        functools.partial(_inproj_kernel, n_keep=n_keep, rows_minor=rows_minor),
        out_shape=([jax.ShapeDtypeStruct((b, l, w), dt) for w, dt in zip(widths, dtypes)]
                   + [kept_shape] * 2),
        grid=(b // bb, nj),
        in_specs=[tok(d), _mod_spec(_SC1, bb, mod_row0), _mod_spec(_SH1, bb, mod_row0),
                  _resident((1, d)), _resident(lb_logits.shape), _resident(w_in_bf16.shape)],
        out_specs=[tok(w) for w in widths] + [kept, kept],
        compiler_params=_params("arbitrary", "arbitrary"),
        name="norm_inproj",
    )(x, mod, mod, norm_g, lb_logits, w_in_bf16)


_HGRN_TILE = 256


def _hgrn_kernel(qa_ref, lf_ref, ka_ref, va_ref, gs_ref, og_ref, s0_ref,
                 o_ref, sfin_ref, st_ref):
    j = pl.program_id(1)
    bb, tb, _ = qa_ref.shape
    th = min(tb, _HGRN_TILE)
    nc = th // CHUNK

    @pl.when(j == 0)
    def _():
        for bi in range(bb):
            for h in range(H_A):
                st_ref[bi, h] = s0_ref[bi, h].T

    row = lax.broadcasted_iota(jnp.int32, (th, th), 0)
    col = lax.broadcasted_iota(jnp.int32, (th, th), 1)
    tri = (row >= col) & (row // CHUNK == col // CHUNK)
    tri_bf = jnp.where(tri, 1.0, 0.0).astype(BF16)
    nt = (((1,), (1,)), ((), ()))
    tn = (((0,), (0,)), ((), ()))
    chunked = lambda a: a.reshape(nc, CHUNK, a.shape[-1])
    flat = lambda a: a.reshape(th, a.shape[-1])

    def seq_tile(t0, bi):
        tile = slice(t0, t0 + th)
        cum = chunked(sum(jnp.dot(tri_bf, piece, preferred_element_type=F32)
                          for piece in _split3(lf_ref[bi, tile, :])[:2]))
        mid = cum[:, CHUNK // 2:CHUNK // 2 + 1]
        tot = cum[:, CHUNK - 1:CHUNK]
        q = chunked(qa_ref[bi, tile, :].astype(F32))
        k = chunked(ka_ref[bi, tile, :].astype(F32))
        v = va_ref[bi, tile, :].astype(BF16)
        qe32 = q * jnp.exp(cum - mid)
        ke32 = k * jnp.exp(mid - cum)
        qe = flat(qe32).astype(BF16)
        ke = flat(ke32).astype(BF16)
        qd = flat(qe32 * jnp.exp(mid)).astype(BF16)
        kd = flat(ke32 * jnp.exp(tot - mid)).astype(BF16)
        decay = jnp.exp(tot)
        yield
        for h in range(H_A):
            lanes = slice(h * DK_A, (h + 1) * DK_A)
            a = lax.dot_general(qe[:, lanes], ke[:, lanes], nt, preferred_element_type=F32)
            a = jnp.where(tri, a, 0.0).astype(BF16)
            intra = jnp.dot(a, v[:, lanes], preferred_element_type=F32)
            st = st_ref[bi, h]
            inter = []
            for c in range(nc):
                rows = slice(c * CHUNK, (c + 1) * CHUNK)
                inter.append(lax.dot_general(qd[rows, lanes], st.astype(BF16), nt,
                                             preferred_element_type=F32))
                st = decay[c, :, lanes] * st + lax.dot_general(
                    v[rows, lanes], kd[rows, lanes], tn, preferred_element_type=F32)
            st_ref[bi, h] = st
            o = intra + jnp.concatenate(inter, axis=0)
            o = o * lax.rsqrt(jnp.mean(o * o, axis=-1, keepdims=True) + EPS)
            gated = o * og_ref[:, lanes] * gs_ref[bi, tile, lanes].astype(F32)
            o_ref[bi, tile, lanes] = gated.astype(o_ref.dtype)
            yield

    _trace_skewed([seq_tile(t0, bi) for t0 in range(0, tb, th) for bi in range(bb)], lead=2)

    @pl.when(j == pl.num_programs(1) - 1)
    def _():
        for bi in range(bb):
            for h in range(H_A):
                sfin_ref[bi, h] = st_ref[bi, h].T


def _hgrn(qa, lf, ka, va, gs, out_norm, s0, bb, th):
    b, l, _ = qa.shape
    tok = pl.BlockSpec((bb, th, K_A), lambda i, j: (i, j, 0))
    state = pl.BlockSpec((bb, H_A, DK_A, DV_A), lambda i, j: (i, 0, 0, 0))
    return pl.pallas_call(
        _hgrn_kernel,
        out_shape=[jax.ShapeDtypeStruct((b, l, V_A), BF16),
                   jax.ShapeDtypeStruct((b, H_A, DK_A, DV_A), F32)],
        grid=(b // bb, l // th),
        in_specs=[tok, tok, tok, tok, tok, _resident((1, V_A)), state],
        out_specs=[tok, state],
        scratch_shapes=[pltpu.VMEM((bb, H_A, DV_A, DK_A), F32)],
        compiler_params=_params("arbitrary", "arbitrary"),
        name="hgrn2_recurrence",
    )(qa, lf, ka, va, gs, out_norm, s0)


_HEADS_PER_UNIT = 4
_UNIT_LANES = _HEADS_PER_UNIT * DH_B
_UNIT_ROWS = _HEADS_PER_UNIT * CHUNK


def _attn_kernel(q_ref, kp_ref, kc_ref, vp_ref, vc_ref, bias_ref, o_ref, kbuf, vbuf, *,
                 mask_first_tile):
    t = pl.program_id(1)
    bb, tq, _ = q_ref.shape
    for prev, cur, buf in ((kp_ref, kc_ref, kbuf), (vp_ref, vc_ref, vbuf)):
        buf[:, 0:WINDOW, :] = prev[...].astype(BF16)
        buf[:, WINDOW:WINDOW + tq, :] = cur[...].astype(BF16)

    lane_head = lax.broadcasted_iota(jnp.int32, (CHUNK, _UNIT_LANES), 1) // DH_B
    own = [lane_head == h for h in range(_HEADS_PER_UNIT)]
    key = lax.broadcasted_iota(jnp.int32, (_UNIT_ROWS, BAND), 1)
    nt = (((1,), (1,)), ((), ()))

    def tile(masked):
        for bi in range(bb):
            for c in range(tq // CHUNK):
                rows = slice(c * CHUNK, (c + 1) * CHUNK)
                band = slice(c * CHUNK, c * CHUNK + BAND)
                for g in range(H_B // _HEADS_PER_UNIT):
                    lanes = slice(g * _UNIT_LANES, (g + 1) * _UNIT_LANES)
                    q4 = q_ref[bi, rows, lanes]
                    qs = jnp.concatenate([jnp.where(m, q4, jnp.zeros_like(q4)) for m in own],
                                         axis=0).astype(BF16)
                    s = lax.dot_general(qs, kbuf[bi, band, lanes], nt,
                                        preferred_element_type=F32)
                    s = s + bias_ref[g]
                    if masked:
                        s = jnp.where(key + c * CHUNK >= WINDOW, s, NEG)
                    p = jnp.exp2(s - jnp.max(s, axis=-1, keepdims=True))
                    inv = 1.0 / jnp.sum(p, axis=-1, keepdims=True)
                    o4 = jnp.dot(p.astype(BF16), vbuf[bi, band, lanes],
                                 preferred_element_type=F32) * inv
                    o = jnp.where(own[0], o4[0:CHUNK], 0.0)
                    for h in range(1, _HEADS_PER_UNIT):
                        o = jnp.where(own[h], o4[h * CHUNK:(h + 1) * CHUNK], o)
                    o_ref[bi, rows, lanes] = o.astype(o_ref.dtype)

    if mask_first_tile:
        pl.when(t == 0)(lambda: tile(True))
        pl.when(t > 0)(lambda: tile(False))
    else:
        tile(False)


def _attn_cache_kernel(q_ref, kt_ref, kc_ref, vt_ref, vc_ref, bias_ref, o_ref):
    bb = q_ref.shape[0]
    lane_head = lax.broadcasted_iota(jnp.int32, (CHUNK, _UNIT_LANES), 1) // DH_B
    own = [lane_head == h for h in range(_HEADS_PER_UNIT)]
    nt = (((1,), (1,)), ((), ()))
    dot = lambda a, b: jnp.dot(a, b, preferred_element_type=F32)
    dot_nt = lambda a, b: lax.dot_general(a, b, nt, preferred_element_type=F32)

    for bi in range(bb):
        for g in range(H_B // _HEADS_PER_UNIT):
            lanes = slice(g * _UNIT_LANES, (g + 1) * _UNIT_LANES)
            heads = slice(g * _HEADS_PER_UNIT, (g + 1) * _HEADS_PER_UNIT)
            kt4 = kt_ref[bi, heads].reshape(_UNIT_LANES, WINDOW).astype(BF16)
            vt4 = vt_ref[bi, heads].reshape(_UNIT_LANES, WINDOW).astype(BF16)
            q4 = q_ref[bi, :, lanes]
            qs = jnp.concatenate([jnp.where(m, q4, jnp.zeros_like(q4)) for m in own],
                                 axis=0).astype(BF16)
            s_old = dot(qs, kt4) + bias_ref[g, :, 0:WINDOW]
            s_new = dot_nt(qs, kc_ref[bi, :, lanes]) + bias_ref[g, :, WINDOW:BAND]
            top = jnp.maximum(jnp.max(s_old, axis=-1, keepdims=True),
                              jnp.max(s_new, axis=-1, keepdims=True))
            p_old = jnp.exp2(s_old - top)
            p_new = jnp.exp2(s_new - top)
            inv = 1.0 / (jnp.sum(p_old, axis=-1, keepdims=True)
                         + jnp.sum(p_new, axis=-1, keepdims=True))
            o4 = (dot_nt(p_old.astype(BF16), vt4)
                  + dot(p_new.astype(BF16), vc_ref[bi, :, lanes])) * inv
            o = jnp.where(own[0], o4[0:CHUNK], 0.0)
            for h in range(1, _HEADS_PER_UNIT):
                o = jnp.where(own[h], o4[h * CHUNK:(h + 1) * CHUNK], o)
            o_ref[bi, :, lanes] = o.astype(o_ref.dtype)


def _attention(q, k, v, cache_kt, cache_vt, bias, bb, tq):
    b, l, _ = q.shape
    cur = pl.BlockSpec((bb, tq, W_B), lambda i, j: (i, j, 0))
    bias = bias.reshape(H_B // _HEADS_PER_UNIT, _UNIT_ROWS, BAND)
    out_shape = jax.ShapeDtypeStruct((b, l, W_B), BF16)
    if cache_kt is None:
        assert tq % WINDOW == 0
        per_tile = tq // WINDOW
        prev = pl.BlockSpec((bb, WINDOW, W_B),
                            lambda i, j: (i, jnp.maximum(j * per_tile - 1, 0), 0))
        return pl.pallas_call(
            functools.partial(_attn_kernel, mask_first_tile=True),
            out_shape=out_shape,
            grid=(b // bb, l // tq),
            in_specs=[cur, prev, cur, prev, cur, _resident(bias.shape)],
            out_specs=cur,
            scratch_shapes=[pltpu.VMEM((bb, WINDOW + tq, W_B), BF16),
                            pltpu.VMEM((bb, WINDOW + tq, W_B), BF16)],
            compiler_params=_params("arbitrary", "arbitrary"),
            name="band_attention",
        )(q, k, k, v, v, bias)
    assert tq == l == CHUNK and cache_kt.shape[1:] == (H_B, DH_B, WINDOW)
    cached = pl.BlockSpec((bb, H_B, DH_B, WINDOW), lambda i, j: (i, 0, 0, 0))
    return pl.pallas_call(
        _attn_cache_kernel,
        out_shape=out_shape,
        grid=(b // bb, 1),
        in_specs=[cur, cached, cur, cached, cur, _resident(bias.shape)],
        out_specs=cur,
        compiler_params=_params("arbitrary", "arbitrary"),
        name="cache_attention",
    )(q, cache_kt, k, cache_vt, v, bias)


def _post_kernel(x_ref, oa_ref, ob_ref, gta_ref, gtb_ref, g1_ref, sc2_ref, sh2_ref, g2_ref,
                 nffn_ref, nfin_ref, wa_ref, wb_ref, wo_ref, wfi_ref, wfo_ref, y_ref):
    bb, tl, d = x_ref.shape

    def tile(bs, ts):
        x = x_ref[bs, ts, :]
        n = x.shape[0] * x.shape[1]
        flat = lambda ref: ref[bs, ts, :].reshape(n, ref.shape[-1])
        per_tok = lambda v: v.reshape(x.shape)

        ya = jnp.dot(flat(oa_ref), wa_ref[...], preferred_element_type=F32)
        yb = jnp.dot(flat(ob_ref), wb_ref[...], preferred_element_type=F32)
        merged = flat(gta_ref).astype(F32) * ya + flat(gtb_ref).astype(F32) * yb
        yield
        y = jnp.dot(merged.astype(BF16), wo_ref[...], preferred_element_type=F32)
        x1 = x + g1_ref[bs] * per_tok(y)
        h2 = _rms(x1, nffn_ref[...]) * (1.0 + sc2_ref[bs]) + sh2_ref[bs]
        hb = h2.reshape(n, d).astype(BF16)
        yield
        acc = jnp.zeros((n, d), F32)
        for lo, hi in _FFN_CHUNKS:
            a = jnp.dot(hb, wfi_ref[:, lo:hi], preferred_element_type=F32)
            u = jnp.dot(hb, wfi_ref[:, D_FF + lo:D_FF + hi], preferred_element_type=F32)
            acc = acc + jnp.dot((_silu(a) * u).astype(BF16), wfo_ref[lo:hi, :],
                                preferred_element_type=F32)
            yield
        x2 = x1 + g2_ref[bs] * per_tok(acc)
        y_ref[bs, ts, :] = _rms(x2, nfin_ref[...])
        yield

    _trace_skewed([tile(bs, ts) for bs, ts in _subtiles(bb, tl)], lead=2)


def _post(x, oa, ob, gta, gtb, mod, mod_row0, norm_ffn, norm_final,
          wa, wb, wo, wfi, wfo, bb, tl):
    b, l, d = x.shape
    tok = lambda w: pl.BlockSpec((bb, tl, w), lambda i, j: (i, j, 0))
    per_b = lambda which: _mod_spec(which, bb, mod_row0)
    return pl.pallas_call(
        _post_kernel,
        out_shape=jax.ShapeDtypeStruct((b, l, d), F32),
        grid=(b // bb, l // tl),
        in_specs=[tok(d), tok(V_A), tok(W_B), tok(d), tok(d),
                  per_b(_G1), per_b(_SC2), per_b(_SH2), per_b(_G2),
                  _resident((1, d)), _resident((1, d)), _resident(wa.shape),
                  _resident(wb.shape), _resident(wo.shape), _resident(wfi.shape),
                  _resident(wfo.shape)],
        out_specs=tok(d),
        compiler_params=pltpu.CompilerParams(
            dimension_semantics=("arbitrary", "arbitrary"),
            vmem_limit_bytes=_POST_VMEM_LIMIT_BYTES),
        name="merge_out_ffn",
    )(x, oa, ob, gta, gtb, mod, mod, mod, mod, norm_ffn, norm_final, wa, wb, wo, wfi, wfo)


def _trunk(x, mod, mod_row0, s0, cache_k, cache_v, bias, w, *, in_tile, post_tile, hgrn_tile,
           attn_tile):
    b, l, d = x.shape

    qa, lf, ka, va, gs, qb, kb, vb, gta, gtb, k_rows, v_rows = _inproj(
        x, mod, mod_row0, w["norm_mix"], w["lb_logits"], w["w_in"], *in_tile)

    oa, s_fin = _hgrn(qa, lf, ka, va, gs, w["out_norm"], s0, *hgrn_tile)
    ob = _attention(qb, kb, vb, cache_k, cache_v, bias, *attn_tile)

    y = _post(x, oa, ob, gta, gtb, mod, mod_row0, w["norm_ffn"], w["norm_final"],
              w["w_branch_a"], w["w_branch_b"], w["w_out"], w["w_ffn_in"], w["w_ffn_out"],
              *post_tile)
    def rows_first(r):
        if r.ndim == 4:
            return r.transpose(0, 3, 1, 2)[None]
        return r.reshape(1, b, r.shape[1] // H_B, H_B, DH_B)

    return y, s_fin[None], rows_first(k_rows), rows_first(v_rows)


def kernel(x_prompt, x_sample, c_prompt, c_sample, state_hgrn, cache_k, cache_v, w_ada, b_ada,
           norm_mix, w_in, hgrn_lb_logits, hgrn_out_norm, w_branch_a, rel_bias, w_branch_b,
           w_out, norm_ffn, w_ffn_in, w_ffn_out, norm_final):
    assert w_ada.shape[0] == 1, "single-layer trunk"
    w = dict(
        w_ada=w_ada[0], b_ada=b_ada[0], norm_mix=norm_mix[0].reshape(1, D_MODEL),
        lb_logits=hgrn_lb_logits, w_in=w_in[0].astype(BF16),
        out_norm=hgrn_out_norm[0].reshape(1, V_A),
        w_branch_a=w_branch_a[0].astype(BF16), w_branch_b=w_branch_b[0].astype(BF16),
        w_out=w_out[0].astype(BF16), norm_ffn=norm_ffn[0].reshape(1, D_MODEL),
        w_ffn_in=w_ffn_in[0].astype(BF16), w_ffn_out=w_ffn_out[0].astype(BF16),
        norm_final=norm_final.reshape(1, D_MODEL),
    )
    bias = _bias_tile(rel_bias[0])

    bp = x_prompt.shape[0]
    bs, ls = x_sample.shape[:2]
    mod = _modulation(jnp.concatenate([c_sample, c_prompt], axis=0), w["w_ada"], w["b_ada"])
    mod = mod.reshape(bs + bp, 1, 6 * D_MODEL)

    s0_prompt = jnp.zeros((bp, H_A, DK_A, DV_A), F32)
    y_p, s_p, k_p, v_p = _trunk(x_prompt, mod, bs, s0_prompt, None, None, bias, w,
                                in_tile=(1, 512), post_tile=(1, 1024), hgrn_tile=(bp, 1024),
                                attn_tile=(1, 2 * WINDOW))
    y_s, s_s, k_s, v_s = _trunk(x_sample, mod, 0, state_hgrn[0],
                                cache_k[0].transpose(0, 2, 3, 1), cache_v[0].transpose(0, 2, 3, 1),
                                bias, w, in_tile=(8, ls), post_tile=(8, ls), hgrn_tile=(8, ls),
                                attn_tile=(4, ls))
    return (y_p, y_s, s_p, k_p, v_p, s_s, k_s, v_s)
```

```python
import functools

import jax
import jax.numpy as jnp
from jax import lax
from jax.experimental import pallas as pl
from jax.experimental.pallas import tpu as pltpu

D_MODEL = 1024
CHUNK = 64
H_A, DK_A, DV_A = 4, 128, 128
K_A = H_A * DK_A
V_A = H_A * DV_A
H_B, DH_B = 8, 64
W_B = H_B * DH_B
N_PAST_CHUNKS = 8
WINDOW = N_PAST_CHUNKS * CHUNK
BAND = WINDOW + CHUNK
REL_CLIP = 128
NUM_REL = CHUNK + REL_CLIP
D_FF = 2816
IN_COLS = 4 * K_A + 3 * W_B + 2 * D_MODEL
EPS = 1e-6
NEG = -1e30
_LOG2E = 1.4426950408889634
_Q_SCALE = (DH_B ** -0.5) * _LOG2E

F32 = jnp.float32
BF16 = jnp.bfloat16

_OFF_QA, _OFF_FA, _OFF_IA, _OFF_GA = 0, K_A, 2 * K_A, 2 * K_A + V_A
_OFF_QB = 2 * K_A + 2 * V_A
_OFF_KB = _OFF_QB + W_B
_OFF_VB = _OFF_KB + W_B
_OFF_GATE_A = _OFF_VB + W_B
_OFF_GATE_B = _OFF_GATE_A + D_MODEL

_VMEM_LIMIT_BYTES = 56 * 1024 * 1024
_LANES = 128
_MXU_DIM = 256
_FFN_CHUNKS = ((0, 6 * _MXU_DIM), (6 * _MXU_DIM, D_FF))
assert D_FF % _MXU_DIM == 0


def _params(*sem):
    return pltpu.CompilerParams(dimension_semantics=sem, vmem_limit_bytes=_VMEM_LIMIT_BYTES)


def _resident(shape):
    zeros = (0,) * len(shape)
    return pl.BlockSpec(shape, lambda *_: zeros, pipeline_mode=pl.Buffered(1))


def _sigmoid(x):
    return 0.5 * jnp.tanh(0.5 * x) + 0.5


def _silu(x):
    return x * _sigmoid(x)


def _rms(x, g):
    return x * lax.rsqrt(jnp.mean(x * x, axis=-1, keepdims=True) + EPS) * g


def _split3(x):
    hi = x.astype(BF16)
    r = x - hi.astype(F32)
    mid = r.astype(BF16)
    lo = (r - mid.astype(F32)).astype(BF16)
    return hi, mid, lo


def _mod_kernel(c_ref, w_ref, b_ref, o_ref):
    s_hi, s_lo, _ = _split3(_silu(c_ref[...]))
    w_hi, w_lo, _ = _split3(w_ref[...])
    dot = lambda a, b: jnp.dot(a, b, preferred_element_type=F32)
    o_ref[...] = dot(s_hi, w_hi) + (dot(s_lo, w_hi) + dot(s_hi, w_lo)) + b_ref[...]


def _modulation(c, w_ada, b_ada):
    b = c.shape[0]
    n_out = w_ada.shape[1]
    return pl.pallas_call(
        _mod_kernel,
        out_shape=jax.ShapeDtypeStruct((b, n_out), F32),
        grid=(n_out // D_MODEL,),
        in_specs=[
            pl.BlockSpec((b, D_MODEL), lambda j: (0, 0)),
            pl.BlockSpec((D_MODEL, D_MODEL), lambda j: (0, j)),
            pl.BlockSpec((1, D_MODEL), lambda j: (0, j)),
        ],
        out_specs=pl.BlockSpec((b, D_MODEL), lambda j: (0, j)),
        compiler_params=_params("arbitrary"),
        name="adaln_mod",
    )(c, w_ada, b_ada.reshape(1, n_out))


_BIAS_EXT = 640


def _bias_kernel(rb_ref, o_ref):
    m = lax.broadcasted_iota(jnp.int32, (NUM_REL, _BIAS_EXT), 1)
    slot = lax.broadcasted_iota(jnp.int32, (NUM_REL, _BIAS_EXT), 0)
    idx = jnp.clip(WINDOW + CHUNK - 1 - m, -(CHUNK - 1), REL_CLIP) + (CHUNK - 1)
    onehot = jnp.where(slot == idx, 1.0, 0.0).astype(BF16)
    ext = sum(jnp.dot(piece, onehot, preferred_element_type=F32)
              for piece in _split3(rb_ref[...]))
    ext = ext * _LOG2E
    for i in range(CHUNK):
        o_ref[i] = ext[:, CHUNK - 1 - i:CHUNK - 1 - i + BAND]


def _bias_tile(rel_bias):
    out = pl.pallas_call(
        _bias_kernel,
        out_shape=jax.ShapeDtypeStruct((CHUNK, H_B, BAND), F32),
        grid=(1,),
        in_specs=[pl.BlockSpec((H_B, NUM_REL), lambda i: (0, 0))],
        out_specs=pl.BlockSpec((CHUNK, H_B, BAND), lambda i: (0, 0, 0)),
        compiler_params=_params("arbitrary"),
        name="rel_bias_tile",
    )(rel_bias)
    return out.transpose(1, 0, 2)


_SUB_ROWS = 256


def _subtiles(bb, tl):
    if bb == 1:
        step = min(tl, _SUB_ROWS)
        return [(slice(0, 1), slice(t, t + step)) for t in range(0, tl, step)]
    step = max(1, min(bb, _SUB_ROWS // tl))
    return [(slice(i, i + step), slice(0, tl)) for i in range(0, bb, step)]


_SH1, _SC1, _G1, _SH2, _SC2, _G2 = range(6)


def _mod_spec(which, bb, row0):
    assert row0 % bb == 0
    return pl.BlockSpec((bb, 1, D_MODEL), lambda i, j: (i + row0 // bb, 0, which))


def _trace_skewed(tiles, lead):
    tiles = list(tiles)
    started, live = 0, []
    while started < len(tiles) or live:
        if started < len(tiles) and (not live or live[-1][1] >= lead):
            live.append([tiles[started], 0])
            started += 1
        for entry in list(live):
            try:
                next(entry[0])
                entry[1] += 1
            except StopIteration:
                live.remove(entry)


def _inproj_kernel(x_ref, sc_ref, sh_ref, g_ref, lbl_ref, w_ref,
                   qa_ref, lf_ref, ka_ref, va_ref, gs_ref, qb_ref, kb_ref, vb_ref,
                   gta_ref, gtb_ref, kf_ref, vf_ref, *, n_keep, rows_minor):
    bb, tl, d = x_ref.shape
    j = pl.program_id(1)
    keep = j >= pl.num_programs(1) - n_keep

    lbl = lbl_ref[...]
    e = jnp.exp(lbl - jnp.max(lbl, axis=0, keepdims=True))
    lb = e[0:1] / jnp.sum(e, axis=0, keepdims=True)

    def tile(bs, ts):
        h = _rms(x_ref[bs, ts, :], g_ref[...]) * (1.0 + sc_ref[bs]) + sh_ref[bs]
        hb = h.reshape(-1, d).astype(BF16)
        nb, nt_ = bs.stop - bs.start, ts.stop - ts.start
        yield

        def proj(off, width):
            return jnp.dot(hb, w_ref[:, off:off + width], preferred_element_type=F32)

        def put(ref, val):
            ref[bs, ts, :] = val.reshape(nb, nt_, -1).astype(ref.dtype)

        put(gta_ref, _sigmoid(proj(_OFF_GATE_A, D_MODEL)))
        yield
        put(gtb_ref, _sigmoid(proj(_OFF_GATE_B, D_MODEL)))
        yield
        put(qa_ref, _silu(proj(_OFF_QA, K_A)))
        f = lb + (1.0 - lb) * _sigmoid(proj(_OFF_FA, K_A))
        put(lf_ref, jnp.log(f))
        put(ka_ref, 1.0 - f)
        yield
        put(gs_ref, _silu(proj(_OFF_GA, V_A)))
        put(va_ref, proj(_OFF_IA, V_A))
        yield
        put(qb_ref, proj(_OFF_QB, W_B) * _Q_SCALE)
        kb = proj(_OFF_KB, W_B)
        vb = proj(_OFF_VB, W_B)
        put(kb_ref, kb)
        put(vb_ref, vb)
        cache_rows.append((bs, ts, kb, vb))
        yield

    cache_rows = []
    _trace_skewed([tile(bs, ts) for bs, ts in _subtiles(bb, tl)], lead=2)

    @pl.when(keep)
    def _():
        for bs, ts, kb, vb in cache_rows:
            nb, nt_ = bs.stop - bs.start, ts.stop - ts.start
            for ref, val in ((kf_ref, kb), (vf_ref, vb)):
                if rows_minor:
                    assert nb == 1
                    ref[bs, :, :, ts] = val.T.reshape(1, H_B, DH_B, nt_)
                else:
                    for hd in range(H_B):
                        ref[bs, pl.ds(ts.start * H_B + hd, nt_, stride=H_B), :] = (
                            val[:, hd * DH_B:(hd + 1) * DH_B].reshape(nb, nt_, DH_B))


def _inproj(x, mod, mod_row0, norm_g, lb_logits, w_in_bf16, bb, tl):
    b, l, d = x.shape
    nj = l // tl
    keep_rows = min(WINDOW, l)
    assert keep_rows % tl == 0
    n_keep = keep_rows // tl
    widths = [K_A, K_A, K_A, V_A, V_A, W_B, W_B, W_B, D_MODEL, D_MODEL]
    dtypes = [BF16, F32, BF16, BF16, BF16, BF16, BF16, BF16, BF16, BF16]
    tok = lambda w: pl.BlockSpec((bb, tl, w), lambda i, j: (i, j, 0))
    kept_block = lambda j: jnp.maximum(j - (nj - n_keep), 0)
    rows_minor = keep_rows >= _LANES
    if rows_minor:
        kept = pl.BlockSpec((bb, H_B, DH_B, tl), lambda i, j: (i, 0, 0, kept_block(j)))
        kept_shape = jax.ShapeDtypeStruct((b, H_B, DH_B, keep_rows), F32)
    else:
        kept = pl.BlockSpec((bb, tl * H_B, DH_B), lambda i, j: (i, kept_block(j), 0))
        kept_shape = jax.ShapeDtypeStruct((b, keep_rows * H_B, DH_B), F32)
    return pl.pallas_call(
        functools.partial(_inproj_kernel, n_keep=n_keep, rows_minor=rows_minor),
        out_shape=([jax.ShapeDtypeStruct((b, l, w), dt) for w, dt in zip(widths, dtypes)]
                   + [kept_shape] * 2),
        grid=(b // bb, nj),
        in_specs=[tok(d), _mod_spec(_SC1, bb, mod_row0), _mod_spec(_SH1, bb, mod_row0),
                  _resident((1, d)), _resident(lb_logits.shape), _resident(w_in_bf16.shape)],
        out_specs=[tok(w) for w in widths] + [kept, kept],
        compiler_params=_params("arbitrary", "arbitrary"),
        name="norm_inproj",
    )(x, mod, mod, norm_g, lb_logits, w_in_bf16)


_HGRN_TILE = 256


def _hgrn_kernel(qa_ref, lf_ref, ka_ref, va_ref, gs_ref, og_ref, s0_ref,
                 o_ref, sfin_ref, st_ref):
    j = pl.program_id(1)
    bb, tb, _ = qa_ref.shape
    th = min(tb, _HGRN_TILE)
    nc = th // CHUNK

    @pl.when(j == 0)
    def _():
        for bi in range(bb):
            for h in range(H_A):
                st_ref[bi, h] = s0_ref[bi, h].T

    row = lax.broadcasted_iota(jnp.int32, (th, th), 0)
    col = lax.broadcasted_iota(jnp.int32, (th, th), 1)
    tri = (row >= col) & (row // CHUNK == col // CHUNK)
    tri_bf = jnp.where(tri, 1.0, 0.0).astype(BF16)
    nt = (((1,), (1,)), ((), ()))
    tn = (((0,), (0,)), ((), ()))
    chunked = lambda a: a.reshape(nc, CHUNK, a.shape[-1])
    flat = lambda a: a.reshape(th, a.shape[-1])

    def seq_tile(t0, bi):
        tile = slice(t0, t0 + th)
        cum = chunked(sum(jnp.dot(tri_bf, piece, preferred_element_type=F32)
                          for piece in _split3(lf_ref[bi, tile, :])[:2]))
        mid = cum[:, CHUNK // 2:CHUNK // 2 + 1]
        tot = cum[:, CHUNK - 1:CHUNK]
        q = chunked(qa_ref[bi, tile, :].astype(F32))
        k = chunked(ka_ref[bi, tile, :].astype(F32))
        v = va_ref[bi, tile, :].astype(BF16)
        qe32 = q * jnp.exp(cum - mid)
        ke32 = k * jnp.exp(mid - cum)
        qe = flat(qe32).astype(BF16)
        ke = flat(ke32).astype(BF16)
        qd = flat(qe32 * jnp.exp(mid)).astype(BF16)
        kd = flat(ke32 * jnp.exp(tot - mid)).astype(BF16)
        decay = jnp.exp(tot)
        yield
        for h in range(H_A):
            lanes = slice(h * DK_A, (h + 1) * DK_A)
            a = lax.dot_general(qe[:, lanes], ke[:, lanes], nt, preferred_element_type=F32)
            a = jnp.where(tri, a, 0.0).astype(BF16)
            intra = jnp.dot(a, v[:, lanes], preferred_element_type=F32)
            st = st_ref[bi, h]
            inter = []
            for c in range(nc):
                rows = slice(c * CHUNK, (c + 1) * CHUNK)
                inter.append(lax.dot_general(qd[rows, lanes], st.astype(BF16), nt,
                                             preferred_element_type=F32))
                st = decay[c, :, lanes] * st + lax.dot_general(
                    v[rows, lanes], kd[rows, lanes], tn, preferred_element_type=F32)
            st_ref[bi, h] = st
            o = intra + jnp.concatenate(inter, axis=0)
            o = o * lax.rsqrt(jnp.mean(o * o, axis=-1, keepdims=True) + EPS)
            gated = o * og_ref[:, lanes] * gs_ref[bi, tile, lanes].astype(F32)
            o_ref[bi, tile, lanes] = gated.astype(o_ref.dtype)
            yield

    _trace_skewed([seq_tile(t0, bi) for t0 in range(0, tb, th) for bi in range(bb)], lead=2)

    @pl.when(j == pl.num_programs(1) - 1)
    def _():
        for bi in range(bb):
            for h in range(H_A):
                sfin_ref[bi, h] = st_ref[bi, h].T


def _hgrn(qa, lf, ka, va, gs, out_norm, s0, bb, th):
    b, l, _ = qa.shape
    tok = pl.BlockSpec((bb, th, K_A), lambda i, j: (i, j, 0))
    state = pl.BlockSpec((bb, H_A, DK_A, DV_A), lambda i, j: (i, 0, 0, 0))
    return pl.pallas_call(
        _hgrn_kernel,
        out_shape=[jax.ShapeDtypeStruct((b, l, V_A), BF16),
                   jax.ShapeDtypeStruct((b, H_A, DK_A, DV_A), F32)],
        grid=(b // bb, l // th),
        in_specs=[tok, tok, tok, tok, tok, _resident((1, V_A)), state],
        out_specs=[tok, state],
        scratch_shapes=[pltpu.VMEM((bb, H_A, DV_A, DK_A), F32)],
        compiler_params=_params("arbitrary", "arbitrary"),
        name="hgrn2_recurrence",
    )(qa, lf, ka, va, gs, out_norm, s0)


_HEADS_PER_UNIT = 4
_UNIT_LANES = _HEADS_PER_UNIT * DH_B
_UNIT_ROWS = _HEADS_PER_UNIT * CHUNK


def _attn_kernel(q_ref, kp_ref, kc_ref, vp_ref, vc_ref, bias_ref, o_ref, kbuf, vbuf, *,
                 mask_first_tile):
    t = pl.program_id(1)
    bb, tq, _ = q_ref.shape
    for prev, cur, buf in ((kp_ref, kc_ref, kbuf), (vp_ref, vc_ref, vbuf)):
        buf[:, 0:WINDOW, :] = prev[...].astype(BF16)
        buf[:, WINDOW:WINDOW + tq, :] = cur[...].astype(BF16)

    lane_head = lax.broadcasted_iota(jnp.int32, (CHUNK, _UNIT_LANES), 1) // DH_B
    own = [lane_head == h for h in range(_HEADS_PER_UNIT)]
    key = lax.broadcasted_iota(jnp.int32, (_UNIT_ROWS, BAND), 1)
    nt = (((1,), (1,)), ((), ()))

    def tile(masked):
        for bi in range(bb):
            for c in range(tq // CHUNK):
                rows = slice(c * CHUNK, (c + 1) * CHUNK)
                band = slice(c * CHUNK, c * CHUNK + BAND)
                for g in range(H_B // _HEADS_PER_UNIT):
                    lanes = slice(g * _UNIT_LANES, (g + 1) * _UNIT_LANES)
                    q4 = q_ref[bi, rows, lanes]
                    qs = jnp.concatenate([jnp.where(m, q4, jnp.zeros_like(q4)) for m in own],
                                         axis=0).astype(BF16)
                    s = lax.dot_general(qs, kbuf[bi, band, lanes], nt,
                                        preferred_element_type=F32)
                    s = s + bias_ref[g]
                    if masked:
                        s = jnp.where(key + c * CHUNK >= WINDOW, s, NEG)
                    p = jnp.exp2(s - jnp.max(s, axis=-1, keepdims=True))
                    inv = 1.0 / jnp.sum(p, axis=-1, keepdims=True)
                    o4 = jnp.dot(p.astype(BF16), vbuf[bi, band, lanes],
                                 preferred_element_type=F32) * inv
                    o = jnp.where(own[0], o4[0:CHUNK], 0.0)
                    for h in range(1, _HEADS_PER_UNIT):
                        o = jnp.where(own[h], o4[h * CHUNK:(h + 1) * CHUNK], o)
                    o_ref[bi, rows, lanes] = o.astype(o_ref.dtype)

    if mask_first_tile:
        pl.when(t == 0)(lambda: tile(True))
        pl.when(t > 0)(lambda: tile(False))
    else:
        tile(False)


def _attn_cache_kernel(q_ref, kt_ref, kc_ref, vt_ref, vc_ref, bias_ref, o_ref):
    bb = q_ref.shape[0]
    lane_head = lax.broadcasted_iota(jnp.int32, (CHUNK, _UNIT_LANES), 1) // DH_B
    own = [lane_head == h for h in range(_HEADS_PER_UNIT)]
    nt = (((1,), (1,)), ((), ()))
    dot = lambda a, b: jnp.dot(a, b, preferred_element_type=F32)
    dot_nt = lambda a, b: lax.dot_general(a, b, nt, preferred_element_type=F32)

    for bi in range(bb):
        for g in range(H_B // _HEADS_PER_UNIT):
            lanes = slice(g * _UNIT_LANES, (g + 1) * _UNIT_LANES)
            heads = slice(g * _HEADS_PER_UNIT, (g + 1) * _HEADS_PER_UNIT)
            kt4 = kt_ref[bi, heads].reshape(_UNIT_LANES, WINDOW).astype(BF16)
            vt4 = vt_ref[bi, heads].reshape(_UNIT_LANES, WINDOW).astype(BF16)
            q4 = q_ref[bi, :, lanes]
            qs = jnp.concatenate([jnp.where(m, q4, jnp.zeros_like(q4)) for m in own],
                                 axis=0).astype(BF16)
            s_old = dot(qs, kt4) + bias_ref[g, :, 0:WINDOW]
            s_new = dot_nt(qs, kc_ref[bi, :, lanes]) + bias_ref[g, :, WINDOW:BAND]
            top = jnp.maximum(jnp.max(s_old, axis=-1, keepdims=True),
                              jnp.max(s_new, axis=-1, keepdims=True))
            p_old = jnp.exp2(s_old - top)
            p_new = jnp.exp2(s_new - top)
            inv = 1.0 / (jnp.sum(p_old, axis=-1, keepdims=True)
                         + jnp.sum(p_new, axis=-1, keepdims=True))
            o4 = (dot_nt(p_old.astype(BF16), vt4)
                  + dot(p_new.astype(BF16), vc_ref[bi, :, lanes])) * inv
            o = jnp.where(own[0], o4[0:CHUNK], 0.0)
            for h in range(1, _HEADS_PER_UNIT):
                o = jnp.where(own[h], o4[h * CHUNK:(h + 1) * CHUNK], o)
            o_ref[bi, :, lanes] = o.astype(o_ref.dtype)


def _attention(q, k, v, cache_kt, cache_vt, bias, bb, tq):
    b, l, _ = q.shape
    cur = pl.BlockSpec((bb, tq, W_B), lambda i, j: (i, j, 0))
    bias = bias.reshape(H_B // _HEADS_PER_UNIT, _UNIT_ROWS, BAND)
    out_shape = jax.ShapeDtypeStruct((b, l, W_B), BF16)
    if cache_kt is None:
        assert tq % WINDOW == 0
        per_tile = tq // WINDOW
        prev = pl.BlockSpec((bb, WINDOW, W_B),
                            lambda i, j: (i, jnp.maximum(j * per_tile - 1, 0), 0))
        return pl.pallas_call(
            functools.partial(_attn_kernel, mask_first_tile=True),
            out_shape=out_shape,
            grid=(b // bb, l // tq),
            in_specs=[cur, prev, cur, prev, cur, _resident(bias.shape)],
            out_specs=cur,
            scratch_shapes=[pltpu.VMEM((bb, WINDOW + tq, W_B), BF16),
                            pltpu.VMEM((bb, WINDOW + tq, W_B), BF16)],
            compiler_params=_params("arbitrary", "arbitrary"),
            name="band_attention",
        )(q, k, k, v, v, bias)
    assert tq == l == CHUNK and cache_kt.shape[1:] == (H_B, DH_B, WINDOW)
    cached = pl.BlockSpec((bb, H_B, DH_B, WINDOW), lambda i, j: (i, 0, 0, 0))
    return pl.pallas_call(
        _attn_cache_kernel,
        out_shape=out_shape,
        grid=(b // bb, 1),
        in_specs=[cur, cached, cur, cached, cur, _resident(bias.shape)],
        out_specs=cur,
        compiler_params=_params("arbitrary", "arbitrary"),
        name="cache_attention",
    )(q, cache_kt, k, cache_vt, v, bias)


def _post_kernel(x_ref, oa_ref, ob_ref, gta_ref, gtb_ref, g1_ref, sc2_ref, sh2_ref, g2_ref,
                 nffn_ref, nfin_ref, wa_ref, wb_ref, wo_ref, wfi_ref, wfo_ref, y_ref):
    bb, tl, d = x_ref.shape

    def tile(bs, ts):
        x = x_ref[bs, ts, :]
        n = x.shape[0] * x.shape[1]
        flat = lambda ref: ref[bs, ts, :].reshape(n, ref.shape[-1])
        per_tok = lambda v: v.reshape(x.shape)

        ya = jnp.dot(flat(oa_ref), wa_ref[...], preferred_element_type=F32)
        yb = jnp.dot(flat(ob_ref), wb_ref[...], preferred_element_type=F32)
        merged = flat(gta_ref).astype(F32) * ya + flat(gtb_ref).astype(F32) * yb
        yield
        y = jnp.dot(merged.astype(BF16), wo_ref[...], preferred_element_type=F32)
        x1 = x + g1_ref[bs] * per_tok(y)
        h2 = _rms(x1, nffn_ref[...]) * (1.0 + sc2_ref[bs]) + sh2_ref[bs]
        hb = h2.reshape(n, d).astype(BF16)
        yield
        acc = jnp.zeros((n, d), F32)
        for lo, hi in _FFN_CHUNKS:
            a = jnp.dot(hb, wfi_ref[:, lo:hi], preferred_element_type=F32)
            u = jnp.dot(hb, wfi_ref[:, D_FF + lo:D_FF + hi], preferred_element_type=F32)
            acc = acc + jnp.dot((_silu(a) * u).astype(BF16), wfo_ref[lo:hi, :],
                                preferred_element_type=F32)
            yield
        x2 = x1 + g2_ref[bs] * per_tok(acc)
        y_ref[bs, ts, :] = _rms(x2, nfin_ref[...])
        yield

    _trace_skewed([tile(bs, ts) for bs, ts in _subtiles(bb, tl)], lead=2)


def _post(x, oa, ob, gta, gtb, mod, mod_row0, norm_ffn, norm_final,
          wa, wb, wo, wfi, wfo, bb, tl):
    b, l, d = x.shape
    tok = lambda w: pl.BlockSpec((bb, tl, w), lambda i, j: (i, j, 0))
    per_b = lambda which: _mod_spec(which, bb, mod_row0)
    return pl.pallas_call(
        _post_kernel,
        out_shape=jax.ShapeDtypeStruct((b, l, d), F32),
        grid=(b // bb, l // tl),
        in_specs=[tok(d), tok(V_A), tok(W_B), tok(d), tok(d),
                  per_b(_G1), per_b(_SC2), per_b(_SH2), per_b(_G2),
                  _resident((1, d)), _resident((1, d)), _resident(wa.shape),
                  _resident(wb.shape), _resident(wo.shape), _resident(wfi.shape),
                  _resident(wfo.shape)],
        out_specs=tok(d),
        compiler_params=_params("arbitrary", "arbitrary"),
        name="merge_out_ffn",
    )(x, oa, ob, gta, gtb, mod, mod, mod, mod, norm_ffn, norm_final, wa, wb, wo, wfi, wfo)


def _trunk(x, mod, mod_row0, s0, cache_k, cache_v, bias, w, *, in_tile, post_tile, hgrn_tile,
           attn_tile):
    b, l, d = x.shape

    qa, lf, ka, va, gs, qb, kb, vb, gta, gtb, k_rows, v_rows = _inproj(
        x, mod, mod_row0, w["norm_mix"], w["lb_logits"], w["w_in"], *in_tile)

    oa, s_fin = _hgrn(qa, lf, ka, va, gs, w["out_norm"], s0, *hgrn_tile)
    ob = _attention(qb, kb, vb, cache_k, cache_v, bias, *attn_tile)

    y = _post(x, oa, ob, gta, gtb, mod, mod_row0, w["norm_ffn"], w["norm_final"],
              w["w_branch_a"], w["w_branch_b"], w["w_out"], w["w_ffn_in"], w["w_ffn_out"],
              *post_tile)
    def rows_first(r):
        if r.ndim == 4:
            return r.transpose(0, 3, 1, 2)[None]
        return r.reshape(1, b, r.shape[1] // H_B, H_B, DH_B)

    return y, s_fin[None], rows_first(k_rows), rows_first(v_rows)


def kernel(x_prompt, x_sample, c_prompt, c_sample, state_hgrn, cache_k, cache_v, w_ada, b_ada,
           norm_mix, w_in, hgrn_lb_logits, hgrn_out_norm, w_branch_a, rel_bias, w_branch_b,
           w_out, norm_ffn, w_ffn_in, w_ffn_out, norm_final):
    assert w_ada.shape[0] == 1, "single-layer trunk"
    w = dict(
        w_ada=w_ada[0], b_ada=b_ada[0], norm_mix=norm_mix[0].reshape(1, D_MODEL),
        lb_logits=hgrn_lb_logits, w_in=w_in[0].astype(BF16),
        out_norm=hgrn_out_norm[0].reshape(1, V_A),
        w_branch_a=w_branch_a[0].astype(BF16), w_branch_b=w_branch_b[0].astype(BF16),
        w_out=w_out[0].astype(BF16), norm_ffn=norm_ffn[0].reshape(1, D_MODEL),
        w_ffn_in=w_ffn_in[0].astype(BF16), w_ffn_out=w_ffn_out[0].astype(BF16),
        norm_final=norm_final.reshape(1, D_MODEL),
    )
    bias = _bias_tile(rel_bias[0])

    bp = x_prompt.shape[0]
    bs, ls = x_sample.shape[:2]
    mod = _modulation(jnp.concatenate([c_sample, c_prompt], axis=0), w["w_ada"], w["b_ada"])
    mod = mod.reshape(bs + bp, 1, 6 * D_MODEL)

    s0_prompt = jnp.zeros((bp, H_A, DK_A, DV_A), F32)
    y_p, s_p, k_p, v_p = _trunk(x_prompt, mod, bs, s0_prompt, None, None, bias, w,
                                in_tile=(1, 512), post_tile=(1, 512), hgrn_tile=(bp, 1024),
                                attn_tile=(1, 2 * WINDOW))
    y_s, s_s, k_s, v_s = _trunk(x_sample, mod, 0, state_hgrn[0],
                                cache_k[0].transpose(0, 2, 3, 1), cache_v[0].transpose(0, 2, 3, 1),
                                bias, w, in_tile=(8, ls), post_tile=(8, ls), hgrn_tile=(8, ls),
                                attn_tile=(4, ls))
    return (y_p, y_s, s_p, k_p, v_p, s_s, k_s, v_s)
```

```python
import functools

import jax
import jax.numpy as jnp
from jax import lax
from jax.experimental import pallas as pl
from jax.experimental.pallas import tpu as pltpu

D_MODEL = 1024
CHUNK = 64
H_A, DK_A, DV_A = 4, 128, 128
K_A = H_A * DK_A
V_A = H_A * DV_A
H_B, DH_B = 8, 64
W_B = H_B * DH_B
N_PAST_CHUNKS = 8
WINDOW = N_PAST_CHUNKS * CHUNK
BAND = WINDOW + CHUNK
REL_CLIP = 128
NUM_REL = CHUNK + REL_CLIP
D_FF = 2816
IN_COLS = 4 * K_A + 3 * W_B + 2 * D_MODEL
EPS = 1e-6
NEG = -1e30
_LOG2E = 1.4426950408889634
_Q_SCALE = (DH_B ** -0.5) * _LOG2E

F32 = jnp.float32
BF16 = jnp.bfloat16

_OFF_QA, _OFF_FA, _OFF_IA, _OFF_GA = 0, K_A, 2 * K_A, 2 * K_A + V_A
_OFF_QB = 2 * K_A + 2 * V_A
_OFF_KB = _OFF_QB + W_B
_OFF_VB = _OFF_KB + W_B
_OFF_GATE_A = _OFF_VB + W_B
_OFF_GATE_B = _OFF_GATE_A + D_MODEL

_VMEM_LIMIT_BYTES = 56 * 1024 * 1024
_LANES = 128
_MXU_DIM = 256
_FFN_CHUNKS = ((0, 6 * _MXU_DIM), (6 * _MXU_DIM, D_FF))
assert D_FF % _MXU_DIM == 0


def _params(*sem):
    return pltpu.CompilerParams(dimension_semantics=sem, vmem_limit_bytes=_VMEM_LIMIT_BYTES)


def _resident(shape):
    zeros = (0,) * len(shape)
    return pl.BlockSpec(shape, lambda *_: zeros, pipeline_mode=pl.Buffered(1))


def _sigmoid(x):
    return 0.5 * jnp.tanh(0.5 * x) + 0.5


def _silu(x):
    return x * _sigmoid(x)


def _rms(x, g):
    return x * lax.rsqrt(jnp.mean(x * x, axis=-1, keepdims=True) + EPS) * g


def _split3(x):
    hi = x.astype(BF16)
    r = x - hi.astype(F32)
    mid = r.astype(BF16)
    lo = (r - mid.astype(F32)).astype(BF16)
    return hi, mid, lo


def _mod_kernel(c_ref, w_ref, b_ref, o_ref):
    s_hi, s_lo, _ = _split3(_silu(c_ref[...]))
    w_hi, w_lo, _ = _split3(w_ref[...])
    dot = lambda a, b: jnp.dot(a, b, preferred_element_type=F32)
    o_ref[...] = dot(s_hi, w_hi) + (dot(s_lo, w_hi) + dot(s_hi, w_lo)) + b_ref[...]


def _modulation(c, w_ada, b_ada):
    b = c.shape[0]
    n_out = w_ada.shape[1]
    return pl.pallas_call(
        _mod_kernel,
        out_shape=jax.ShapeDtypeStruct((b, n_out), F32),
        grid=(n_out // D_MODEL,),
        in_specs=[
            pl.BlockSpec((b, D_MODEL), lambda j: (0, 0)),
            pl.BlockSpec((D_MODEL, D_MODEL), lambda j: (0, j)),
            pl.BlockSpec((1, D_MODEL), lambda j: (0, j)),
        ],
        out_specs=pl.BlockSpec((b, D_MODEL), lambda j: (0, j)),
        compiler_params=_params("arbitrary"),
        name="adaln_mod",
    )(c, w_ada, b_ada.reshape(1, n_out))


_BIAS_EXT = 640


def _bias_kernel(rb_ref, o_ref):
    m = lax.broadcasted_iota(jnp.int32, (NUM_REL, _BIAS_EXT), 1)
    slot = lax.broadcasted_iota(jnp.int32, (NUM_REL, _BIAS_EXT), 0)
    idx = jnp.clip(WINDOW + CHUNK - 1 - m, -(CHUNK - 1), REL_CLIP) + (CHUNK - 1)
    onehot = jnp.where(slot == idx, 1.0, 0.0).astype(BF16)
    ext = sum(jnp.dot(piece, onehot, preferred_element_type=F32)
              for piece in _split3(rb_ref[...]))
    ext = ext * _LOG2E
    for i in range(CHUNK):
        o_ref[i] = ext[:, CHUNK - 1 - i:CHUNK - 1 - i + BAND]


def _bias_tile(rel_bias):
    out = pl.pallas_call(
        _bias_kernel,
        out_shape=jax.ShapeDtypeStruct((CHUNK, H_B, BAND), F32),
        grid=(1,),
        in_specs=[pl.BlockSpec((H_B, NUM_REL), lambda i: (0, 0))],
        out_specs=pl.BlockSpec((CHUNK, H_B, BAND), lambda i: (0, 0, 0)),
        compiler_params=_params("arbitrary"),
        name="rel_bias_tile",
    )(rel_bias)
    return out.transpose(1, 0, 2)


_SUB_ROWS = 256


def _subtiles(bb, tl):
    if bb == 1:
        step = min(tl, _SUB_ROWS)
        return [(slice(0, 1), slice(t, t + step)) for t in range(0, tl, step)]
    step = max(1, min(bb, _SUB_ROWS // tl))
    return [(slice(i, i + step), slice(0, tl)) for i in range(0, bb, step)]


_SH1, _SC1, _G1, _SH2, _SC2, _G2 = range(6)


def _mod_spec(which, bb, row0):
    assert row0 % bb == 0
    return pl.BlockSpec((bb, 1, D_MODEL), lambda i, j: (i + row0 // bb, 0, which))


def _trace_skewed(tiles, lead):
    tiles = list(tiles)
    started, live = 0, []
    while started < len(tiles) or live:
        if started < len(tiles) and (not live or live[-1][1] >= lead):
            live.append([tiles[started], 0])
            started += 1
        for entry in list(live):
            try:
                next(entry[0])
                entry[1] += 1
            except StopIteration:
                live.remove(entry)


def _inproj_kernel(x_ref, sc_ref, sh_ref, g_ref, lbl_ref, w_ref,
                   qa_ref, lf_ref, ka_ref, va_ref, gs_ref, qb_ref, kb_ref, vb_ref,
                   gta_ref, gtb_ref, kf_ref, vf_ref, *, n_keep, rows_minor):
    bb, tl, d = x_ref.shape
    j = pl.program_id(1)
    keep = j >= pl.num_programs(1) - n_keep

    lbl = lbl_ref[...]
    e = jnp.exp(lbl - jnp.max(lbl, axis=0, keepdims=True))
    lb = e[0:1] / jnp.sum(e, axis=0, keepdims=True)

    def tile(bs, ts):
        h = _rms(x_ref[bs, ts, :], g_ref[...]) * (1.0 + sc_ref[bs]) + sh_ref[bs]
        hb = h.reshape(-1, d).astype(BF16)
        nb, nt_ = bs.stop - bs.start, ts.stop - ts.start
        yield

        def proj(off, width):
            return jnp.dot(hb, w_ref[:, off:off + width], preferred_element_type=F32)

        def put(ref, val):
            ref[bs, ts, :] = val.reshape(nb, nt_, -1).astype(ref.dtype)

        put(gta_ref, _sigmoid(proj(_OFF_GATE_A, D_MODEL)))
        yield
        put(gtb_ref, _sigmoid(proj(_OFF_GATE_B, D_MODEL)))
        yield
        put(qa_ref, _silu(proj(_OFF_QA, K_A)))
        f = lb + (1.0 - lb) * _sigmoid(proj(_OFF_FA, K_A))
        put(lf_ref, jnp.log(f))
        put(ka_ref, 1.0 - f)
        yield
        put(gs_ref, _silu(proj(_OFF_GA, V_A)))
        put(va_ref, proj(_OFF_IA, V_A))
        yield
        put(qb_ref, proj(_OFF_QB, W_B) * _Q_SCALE)
        kb = proj(_OFF_KB, W_B)
        vb = proj(_OFF_VB, W_B)
        put(kb_ref, kb)
        put(vb_ref, vb)
        cache_rows.append((bs, ts, kb, vb))
        yield

    cache_rows = []
    _trace_skewed([tile(bs, ts) for bs, ts in _subtiles(bb, tl)], lead=1)

    @pl.when(keep)
    def _():
        for bs, ts, kb, vb in cache_rows:
            nb, nt_ = bs.stop - bs.start, ts.stop - ts.start
            for ref, val in ((kf_ref, kb), (vf_ref, vb)):
                if rows_minor:
                    assert nb == 1
                    ref[bs, :, :, ts] = val.T.reshape(1, H_B, DH_B, nt_)
                else:
                    for hd in range(H_B):
                        ref[bs, pl.ds(ts.start * H_B + hd, nt_, stride=H_B), :] = (
                            val[:, hd * DH_B:(hd + 1) * DH_B].reshape(nb, nt_, DH_B))


def _inproj(x, mod, mod_row0, norm_g, lb_logits, w_in_bf16, bb, tl):
    b, l, d = x.shape
    nj = l // tl
    keep_rows = min(WINDOW, l)
    assert keep_rows % tl == 0
    n_keep = keep_rows // tl
    widths = [K_A, K_A, K_A, V_A, V_A, W_B, W_B, W_B, D_MODEL, D_MODEL]
    dtypes = [BF16, F32, BF16, BF16, BF16, BF16, BF16, BF16, BF16, BF16]
    tok = lambda w: pl.BlockSpec((bb, tl, w), lambda i, j: (i, j, 0))
    kept_block = lambda j: jnp.maximum(j - (nj - n_keep), 0)
    rows_minor = keep_rows >= _LANES
    if rows_minor:
        kept = pl.BlockSpec((bb, H_B, DH_B, tl), lambda i, j: (i, 0, 0, kept_block(j)))
        kept_shape = jax.ShapeDtypeStruct((b, H_B, DH_B, keep_rows), F32)
    else:
        kept = pl.BlockSpec((bb, tl * H_B, DH_B), lambda i, j: (i, kept_block(j), 0))
        kept_shape = jax.ShapeDtypeStruct((b, keep_rows * H_B, DH_B), F32)
    return pl.pallas_call(
        functools.partial(_inproj_kernel, n_keep=n_keep, rows_minor=rows_minor),
        out_shape=([jax.ShapeDtypeStruct((b, l, w), dt) for w, dt in zip(widths, dtypes)]
                   + [kept_shape] * 2),
        grid=(b // bb, nj),
        in_specs=[tok(d), _mod_spec(_SC1, bb, mod_row0), _mod_spec(_SH1, bb, mod_row0),
                  _resident((1, d)), _resident(lb_logits.shape), _resident(w_in_bf16.shape)],
        out_specs=[tok(w) for w in widths] + [kept, kept],
        compiler_params=_params("arbitrary", "arbitrary"),
        name="norm_inproj",
    )(x, mod, mod, norm_g, lb_logits, w_in_bf16)


_HGRN_TILE = 256


def _hgrn_kernel(qa_ref, lf_ref, ka_ref, va_ref, gs_ref, og_ref, s0_ref,
                 o_ref, sfin_ref, st_ref):
    j = pl.program_id(1)
    bb, tb, _ = qa_ref.shape
    th = min(tb, _HGRN_TILE)
    nc = th // CHUNK

    @pl.when(j == 0)
    def _():
        for bi in range(bb):
            for h in range(H_A):
                st_ref[bi, h] = s0_ref[bi, h].T

    row = lax.broadcasted_iota(jnp.int32, (th, th), 0)
    col = lax.broadcasted_iota(jnp.int32, (th, th), 1)
    tri = (row >= col) & (row // CHUNK == col // CHUNK)
    tri_bf = jnp.where(tri, 1.0, 0.0).astype(BF16)
    nt = (((1,), (1,)), ((), ()))
    tn = (((0,), (0,)), ((), ()))
    chunked = lambda a: a.reshape(nc, CHUNK, a.shape[-1])
    flat = lambda a: a.reshape(th, a.shape[-1])

    def seq_tile(t0, bi):
        tile = slice(t0, t0 + th)
        cum = chunked(sum(jnp.dot(tri_bf, piece, preferred_element_type=F32)
                          for piece in _split3(lf_ref[bi, tile, :])[:2]))
        mid = cum[:, CHUNK // 2:CHUNK // 2 + 1]
        tot = cum[:, CHUNK - 1:CHUNK]
        q = chunked(qa_ref[bi, tile, :].astype(F32))
        k = chunked(ka_ref[bi, tile, :].astype(F32))
        v = va_ref[bi, tile, :].astype(BF16)
        qe32 = q * jnp.exp(cum - mid)
        ke32 = k * jnp.exp(mid - cum)
        qe = flat(qe32).astype(BF16)
        ke = flat(ke32).astype(BF16)
        qd = flat(qe32 * jnp.exp(mid)).astype(BF16)
        kd = flat(ke32 * jnp.exp(tot - mid)).astype(BF16)
        decay = jnp.exp(tot)
        yield
        for h in range(H_A):
            lanes = slice(h * DK_A, (h + 1) * DK_A)
            a = lax.dot_general(qe[:, lanes], ke[:, lanes], nt, preferred_element_type=F32)
            a = jnp.where(tri, a, 0.0).astype(BF16)
            intra = jnp.dot(a, v[:, lanes], preferred_element_type=F32)
            st = st_ref[bi, h]
            inter = []
            for c in range(nc):
                rows = slice(c * CHUNK, (c + 1) * CHUNK)
                inter.append(lax.dot_general(qd[rows, lanes], st.astype(BF16), nt,
                                             preferred_element_type=F32))
                st = decay[c, :, lanes] * st + lax.dot_general(
                    v[rows, lanes], kd[rows, lanes], tn, preferred_element_type=F32)
            st_ref[bi, h] = st
            o = intra + jnp.concatenate(inter, axis=0)
            o = o * lax.rsqrt(jnp.mean(o * o, axis=-1, keepdims=True) + EPS)
            gated = o * og_ref[:, lanes] * gs_ref[bi, tile, lanes].astype(F32)
            o_ref[bi, tile, lanes] = gated.astype(o_ref.dtype)
            yield

    _trace_skewed([seq_tile(t0, bi) for t0 in range(0, tb, th) for bi in range(bb)], lead=2)

    @pl.when(j == pl.num_programs(1) - 1)
    def _():
        for bi in range(bb):
            for h in range(H_A):
                sfin_ref[bi, h] = st_ref[bi, h].T


def _hgrn(qa, lf, ka, va, gs, out_norm, s0, bb, th):
    b, l, _ = qa.shape
    tok = pl.BlockSpec((bb, th, K_A), lambda i, j: (i, j, 0))
    state = pl.BlockSpec((bb, H_A, DK_A, DV_A), lambda i, j: (i, 0, 0, 0))
    return pl.pallas_call(
        _hgrn_kernel,
        out_shape=[jax.ShapeDtypeStruct((b, l, V_A), BF16),
                   jax.ShapeDtypeStruct((b, H_A, DK_A, DV_A), F32)],
        grid=(b // bb, l // th),
        in_specs=[tok, tok, tok, tok, tok, _resident((1, V_A)), state],
        out_specs=[tok, state],
        scratch_shapes=[pltpu.VMEM((bb, H_A, DV_A, DK_A), F32)],
        compiler_params=_params("arbitrary", "arbitrary"),
        name="hgrn2_recurrence",
    )(qa, lf, ka, va, gs, out_norm, s0)


_HEADS_PER_UNIT = 4
_UNIT_LANES = _HEADS_PER_UNIT * DH_B
_UNIT_ROWS = _HEADS_PER_UNIT * CHUNK


def _attn_kernel(q_ref, kp_ref, kc_ref, vp_ref, vc_ref, bias_ref, o_ref, kbuf, vbuf, *,
                 mask_first_tile):
    t = pl.program_id(1)
    bb, tq, _ = q_ref.shape
    for prev, cur, buf in ((kp_ref, kc_ref, kbuf), (vp_ref, vc_ref, vbuf)):
        buf[:, 0:WINDOW, :] = prev[...].astype(BF16)
        buf[:, WINDOW:WINDOW + tq, :] = cur[...].astype(BF16)

    lane_head = lax.broadcasted_iota(jnp.int32, (CHUNK, _UNIT_LANES), 1) // DH_B
    own = [lane_head == h for h in range(_HEADS_PER_UNIT)]
    key = lax.broadcasted_iota(jnp.int32, (_UNIT_ROWS, BAND), 1)
    nt = (((1,), (1,)), ((), ()))

    def tile(masked):
        for bi in range(bb):
            for c in range(tq // CHUNK):
                rows = slice(c * CHUNK, (c + 1) * CHUNK)
                band = slice(c * CHUNK, c * CHUNK + BAND)
                for g in range(H_B // _HEADS_PER_UNIT):
                    lanes = slice(g * _UNIT_LANES, (g + 1) * _UNIT_LANES)
                    q4 = q_ref[bi, rows, lanes]
                    qs = jnp.concatenate([jnp.where(m, q4, jnp.zeros_like(q4)) for m in own],
                                         axis=0).astype(BF16)
                    s = lax.dot_general(qs, kbuf[bi, band, lanes], nt,
                                        preferred_element_type=F32)
                    s = s + bias_ref[g]
                    if masked:
                        s = jnp.where(key + c * CHUNK >= WINDOW, s, NEG)
                    p = jnp.exp2(s - jnp.max(s, axis=-1, keepdims=True))
                    inv = 1.0 / jnp.sum(p, axis=-1, keepdims=True)
                    o4 = jnp.dot(p.astype(BF16), vbuf[bi, band, lanes],
                                 preferred_element_type=F32) * inv
                    o = jnp.where(own[0], o4[0:CHUNK], 0.0)
                    for h in range(1, _HEADS_PER_UNIT):
                        o = jnp.where(own[h], o4[h * CHUNK:(h + 1) * CHUNK], o)
                    o_ref[bi, rows, lanes] = o.astype(o_ref.dtype)

    if mask_first_tile:
        pl.when(t == 0)(lambda: tile(True))
        pl.when(t > 0)(lambda: tile(False))
    else:
        tile(False)


def _attn_cache_kernel(q_ref, kt_ref, kc_ref, vt_ref, vc_ref, bias_ref, o_ref):
    bb = q_ref.shape[0]
    lane_head = lax.broadcasted_iota(jnp.int32, (CHUNK, _UNIT_LANES), 1) // DH_B
    own = [lane_head == h for h in range(_HEADS_PER_UNIT)]
    nt = (((1,), (1,)), ((), ()))
    dot = lambda a, b: jnp.dot(a, b, preferred_element_type=F32)
    dot_nt = lambda a, b: lax.dot_general(a, b, nt, preferred_element_type=F32)

    for bi in range(bb):
        for g in range(H_B // _HEADS_PER_UNIT):
            lanes = slice(g * _UNIT_LANES, (g + 1) * _UNIT_LANES)
            heads = slice(g * _HEADS_PER_UNIT, (g + 1) * _HEADS_PER_UNIT)
            kt4 = kt_ref[bi, heads].reshape(_UNIT_LANES, WINDOW).astype(BF16)
            vt4 = vt_ref[bi, heads].reshape(_UNIT_LANES, WINDOW).astype(BF16)
            q4 = q_ref[bi, :, lanes]
            qs = jnp.concatenate([jnp.where(m, q4, jnp.zeros_like(q4)) for m in own],
                                 axis=0).astype(BF16)
            s_old = dot(qs, kt4) + bias_ref[g, :, 0:WINDOW]
            s_new = dot_nt(qs, kc_ref[bi, :, lanes]) + bias_ref[g, :, WINDOW:BAND]
            top = jnp.maximum(jnp.max(s_old, axis=-1, keepdims=True),
                              jnp.max(s_new, axis=-1, keepdims=True))
            p_old = jnp.exp2(s_old - top)
            p_new = jnp.exp2(s_new - top)
            inv = 1.0 / (jnp.sum(p_old, axis=-1, keepdims=True)
                         + jnp.sum(p_new, axis=-1, keepdims=True))
            o4 = (dot_nt(p_old.astype(BF16), vt4)
                  + dot(p_new.astype(BF16), vc_ref[bi, :, lanes])) * inv
            o = jnp.where(own[0], o4[0:CHUNK], 0.0)
            for h in range(1, _HEADS_PER_UNIT):
                o = jnp.where(own[h], o4[h * CHUNK:(h + 1) * CHUNK], o)
            o_ref[bi, :, lanes] = o.astype(o_ref.dtype)


def _attention(q, k, v, cache_kt, cache_vt, bias, bb, tq):
    b, l, _ = q.shape
    cur = pl.BlockSpec((bb, tq, W_B), lambda i, j: (i, j, 0))
    bias = bias.reshape(H_B // _HEADS_PER_UNIT, _UNIT_ROWS, BAND)
    out_shape = jax.ShapeDtypeStruct((b, l, W_B), BF16)
    if cache_kt is None:
        assert tq % WINDOW == 0
        per_tile = tq // WINDOW
        prev = pl.BlockSpec((bb, WINDOW, W_B),
                            lambda i, j: (i, jnp.maximum(j * per_tile - 1, 0), 0))
        return pl.pallas_call(
            functools.partial(_attn_kernel, mask_first_tile=True),
            out_shape=out_shape,
            grid=(b // bb, l // tq),
            in_specs=[cur, prev, cur, prev, cur, _resident(bias.shape)],
            out_specs=cur,
            scratch_shapes=[pltpu.VMEM((bb, WINDOW + tq, W_B), BF16),
                            pltpu.VMEM((bb, WINDOW + tq, W_B), BF16)],
            compiler_params=_params("arbitrary", "arbitrary"),
            name="band_attention",
        )(q, k, k, v, v, bias)
    assert tq == l == CHUNK and cache_kt.shape[1:] == (H_B, DH_B, WINDOW)
    cached = pl.BlockSpec((bb, H_B, DH_B, WINDOW), lambda i, j: (i, 0, 0, 0))
    return pl.pallas_call(
        _attn_cache_kernel,
        out_shape=out_shape,
        grid=(b // bb, 1),
        in_specs=[cur, cached, cur, cached, cur, _resident(bias.shape)],
        out_specs=cur,
        compiler_params=_params("arbitrary", "arbitrary"),
        name="cache_attention",
    )(q, cache_kt, k, cache_vt, v, bias)


def _post_kernel(x_ref, oa_ref, ob_ref, gta_ref, gtb_ref, g1_ref, sc2_ref, sh2_ref, g2_ref,
                 nffn_ref, nfin_ref, wa_ref, wb_ref, wo_ref, wfi_ref, wfo_ref, y_ref):
    bb, tl, d = x_ref.shape

    def tile(bs, ts):
        x = x_ref[bs, ts, :]
        n = x.shape[0] * x.shape[1]
        flat = lambda ref: ref[bs, ts, :].reshape(n, ref.shape[-1])
        per_tok = lambda v: v.reshape(x.shape)

        ya = jnp.dot(flat(oa_ref), wa_ref[...], preferred_element_type=F32)
        yb = jnp.dot(flat(ob_ref), wb_ref[...], preferred_element_type=F32)
        merged = flat(gta_ref).astype(F32) * ya + flat(gtb_ref).astype(F32) * yb
        yield
        y = jnp.dot(merged.astype(BF16), wo_ref[...], preferred_element_type=F32)
        x1 = x + g1_ref[bs] * per_tok(y)
        h2 = _rms(x1, nffn_ref[...]) * (1.0 + sc2_ref[bs]) + sh2_ref[bs]
        hb = h2.reshape(n, d).astype(BF16)
        yield
        acc = jnp.zeros((n, d), F32)
        for lo, hi in _FFN_CHUNKS:
            a = jnp.dot(hb, wfi_ref[:, lo:hi], preferred_element_type=F32)
            u = jnp.dot(hb, wfi_ref[:, D_FF + lo:D_FF + hi], preferred_element_type=F32)
            acc = acc + jnp.dot((_silu(a) * u).astype(BF16), wfo_ref[lo:hi, :],
                                preferred_element_type=F32)
            yield
        x2 = x1 + g2_ref[bs] * per_tok(acc)
        y_ref[bs, ts, :] = _rms(x2, nfin_ref[...])
        yield

    _trace_skewed([tile(bs, ts) for bs, ts in _subtiles(bb, tl)], lead=1)


def _post(x, oa, ob, gta, gtb, mod, mod_row0, norm_ffn, norm_final,
          wa, wb, wo, wfi, wfo, bb, tl):
    b, l, d = x.shape
    tok = lambda w: pl.BlockSpec((bb, tl, w), lambda i, j: (i, j, 0))
    per_b = lambda which: _mod_spec(which, bb, mod_row0)
    return pl.pallas_call(
        _post_kernel,
        out_shape=jax.ShapeDtypeStruct((b, l, d), F32),
        grid=(b // bb, l // tl),
        in_specs=[tok(d), tok(V_A), tok(W_B), tok(d), tok(d),
                  per_b(_G1), per_b(_SC2), per_b(_SH2), per_b(_G2),
                  _resident((1, d)), _resident((1, d)), _resident(wa.shape),
                  _resident(wb.shape), _resident(wo.shape), _resident(wfi.shape),
                  _resident(wfo.shape)],
        out_specs=tok(d),
        compiler_params=_params("arbitrary", "arbitrary"),
        name="merge_out_ffn",
    )(x, oa, ob, gta, gtb, mod, mod, mod, mod, norm_ffn, norm_final, wa, wb, wo, wfi, wfo)


def _trunk(x, mod, mod_row0, s0, cache_k, cache_v, bias, w, *, in_tile, post_tile, hgrn_tile,
           attn_tile):
    b, l, d = x.shape

    qa, lf, ka, va, gs, qb, kb, vb, gta, gtb, k_rows, v_rows = _inproj(
        x, mod, mod_row0, w["norm_mix"], w["lb_logits"], w["w_in"], *in_tile)

    oa, s_fin = _hgrn(qa, lf, ka, va, gs, w["out_norm"], s0, *hgrn_tile)
    ob = _attention(qb, kb, vb, cache_k, cache_v, bias, *attn_tile)

    y = _post(x, oa, ob, gta, gtb, mod, mod_row0, w["norm_ffn"], w["norm_final"],
              w["w_branch_a"], w["w_branch_b"], w["w_out"], w["w_ffn_in"], w["w_ffn_out"],
              *post_tile)
    def rows_first(r):
        if r.ndim == 4:
            return r.transpose(0, 3, 1, 2)[None]
        return r.reshape(1, b, r.shape[1] // H_B, H_B, DH_B)

    return y, s_fin[None], rows_first(k_rows), rows_first(v_rows)


def kernel(x_prompt, x_sample, c_prompt, c_sample, state_hgrn, cache_k, cache_v, w_ada, b_ada,
           norm_mix, w_in, hgrn_lb_logits, hgrn_out_norm, w_branch_a, rel_bias, w_branch_b,
           w_out, norm_ffn, w_ffn_in, w_ffn_out, norm_final):
    assert w_ada.shape[0] == 1, "single-layer trunk"
    w = dict(
        w_ada=w_ada[0], b_ada=b_ada[0], norm_mix=norm_mix[0].reshape(1, D_MODEL),
        lb_logits=hgrn_lb_logits, w_in=w_in[0].astype(BF16),
        out_norm=hgrn_out_norm[0].reshape(1, V_A),
        w_branch_a=w_branch_a[0].astype(BF16), w_branch_b=w_branch_b[0].astype(BF16),
        w_out=w_out[0].astype(BF16), norm_ffn=norm_ffn[0].reshape(1, D_MODEL),
        w_ffn_in=w_ffn_in[0].astype(BF16), w_ffn_out=w_ffn_out[0].astype(BF16),
        norm_final=norm_final.reshape(1, D_MODEL),
    )
    bias = _bias_tile(rel_bias[0])

    bp = x_prompt.shape[0]
    bs, ls = x_sample.shape[:2]
    mod = _modulation(jnp.concatenate([c_sample, c_prompt], axis=0), w["w_ada"], w["b_ada"])
    mod = mod.reshape(bs + bp, 1, 6 * D_MODEL)

    s0_prompt = jnp.zeros((bp, H_A, DK_A, DV_A), F32)
    y_p, s_p, k_p, v_p = _trunk(x_prompt, mod, bs, s0_prompt, None, None, bias, w,
                                in_tile=(1, 512), post_tile=(1, 512), hgrn_tile=(bp, 1024),
                                attn_tile=(1, 2 * WINDOW))
    y_s, s_s, k_s, v_s = _trunk(x_sample, mod, 0, state_hgrn[0],
                                cache_k[0].transpose(0, 2, 3, 1), cache_v[0].transpose(0, 2, 3, 1),
                                bias, w, in_tile=(8, ls), post_tile=(8, ls), hgrn_tile=(8, ls),
                                attn_tile=(4, ls))
    return (y_p, y_s, s_p, k_p, v_p, s_s, k_s, v_s)
```

```python
import functools

import jax
import jax.numpy as jnp
from jax import lax
from jax.experimental import pallas as pl
from jax.experimental.pallas import tpu as pltpu

D_MODEL = 1024
CHUNK = 64
H_A, DK_A, DV_A = 4, 128, 128
K_A = H_A * DK_A
V_A = H_A * DV_A
H_B, DH_B = 8, 64
W_B = H_B * DH_B
N_PAST_CHUNKS = 8
WINDOW = N_PAST_CHUNKS * CHUNK
BAND = WINDOW + CHUNK
REL_CLIP = 128
NUM_REL = CHUNK + REL_CLIP
D_FF = 2816
EPS = 1e-6
NEG = -1e30
_LOG2E = 1.4426950408889634
_Q_SCALE = (DH_B ** -0.5) * _LOG2E

F32 = jnp.float32
BF16 = jnp.bfloat16

_OFF_QA, _OFF_FA, _OFF_IA, _OFF_GA = 0, K_A, 2 * K_A, 2 * K_A + V_A
_OFF_QB = 2 * K_A + 2 * V_A
_OFF_KB = _OFF_QB + W_B
_OFF_VB = _OFF_KB + W_B
_OFF_GATE_A = _OFF_VB + W_B
_OFF_GATE_B = _OFF_GATE_A + D_MODEL

_VMEM_LIMIT_BYTES = 56 * 1024 * 1024
_LANES = 128
_MXU_DIM = 256
_FFN_CHUNKS = ((0, 6 * _MXU_DIM), (6 * _MXU_DIM, D_FF))
assert D_FF % _MXU_DIM == 0


def _params(*sem):
    return pltpu.CompilerParams(dimension_semantics=sem, vmem_limit_bytes=_VMEM_LIMIT_BYTES)


def _resident(shape):
    zeros = (0,) * len(shape)
    return pl.BlockSpec(shape, lambda *_: zeros, pipeline_mode=pl.Buffered(1))


def _sigmoid(x):
    return 0.5 * jnp.tanh(0.5 * x) + 0.5


def _silu(x):
    return x * _sigmoid(x)


def _rms(x, g):
    return x * lax.rsqrt(jnp.mean(x * x, axis=-1, keepdims=True) + EPS) * g


def _split3(x):
    hi = x.astype(BF16)
    r = x - hi.astype(F32)
    mid = r.astype(BF16)
    lo = (r - mid.astype(F32)).astype(BF16)
    return hi, mid, lo


def _mod_kernel(c_ref, w_ref, b_ref, o_ref):
    s_hi, s_lo, _ = _split3(_silu(c_ref[...]))
    w_hi, w_lo, _ = _split3(w_ref[...])
    dot = lambda a, b: jnp.dot(a, b, preferred_element_type=F32)
    o_ref[...] = dot(s_hi, w_hi) + (dot(s_lo, w_hi) + dot(s_hi, w_lo)) + b_ref[...]


def _modulation(c, w_ada, b_ada):
    b = c.shape[0]
    n_out = w_ada.shape[1]
    return pl.pallas_call(
        _mod_kernel,
        out_shape=jax.ShapeDtypeStruct((b, n_out), F32),
        grid=(n_out // D_MODEL,),
        in_specs=[
            pl.BlockSpec((b, D_MODEL), lambda j: (0, 0)),
            pl.BlockSpec((D_MODEL, D_MODEL), lambda j: (0, j)),
            pl.BlockSpec((1, D_MODEL), lambda j: (0, j)),
        ],
        out_specs=pl.BlockSpec((b, D_MODEL), lambda j: (0, j)),
        compiler_params=_params("arbitrary"),
        name="adaln_mod",
    )(c, w_ada, b_ada.reshape(1, n_out))


_BIAS_EXT = 640


def _bias_kernel(rb_ref, o_ref):
    m = lax.broadcasted_iota(jnp.int32, (NUM_REL, _BIAS_EXT), 1)
    slot = lax.broadcasted_iota(jnp.int32, (NUM_REL, _BIAS_EXT), 0)
    idx = jnp.clip(WINDOW + CHUNK - 1 - m, -(CHUNK - 1), REL_CLIP) + (CHUNK - 1)
    onehot = jnp.where(slot == idx, 1.0, 0.0).astype(BF16)
    ext = sum(jnp.dot(piece, onehot, preferred_element_type=F32)
              for piece in _split3(rb_ref[...]))
    ext = ext * _LOG2E
    for i in range(CHUNK):
        o_ref[i] = ext[:, CHUNK - 1 - i:CHUNK - 1 - i + BAND]


def _bias_tile(rel_bias):
    out = pl.pallas_call(
        _bias_kernel,
        out_shape=jax.ShapeDtypeStruct((CHUNK, H_B, BAND), F32),
        grid=(1,),
        in_specs=[pl.BlockSpec((H_B, NUM_REL), lambda i: (0, 0))],
        out_specs=pl.BlockSpec((CHUNK, H_B, BAND), lambda i: (0, 0, 0)),
        compiler_params=_params("arbitrary"),
        name="rel_bias_tile",
    )(rel_bias)
    return out.transpose(1, 0, 2)


_SUB_ROWS = 256


def _subtiles(bb, tl):
    if bb == 1:
        step = min(tl, _SUB_ROWS)
        return [(slice(0, 1), slice(t, t + step)) for t in range(0, tl, step)]
    step = max(1, min(bb, _SUB_ROWS // tl))
    return [(slice(i, i + step), slice(0, tl)) for i in range(0, bb, step)]


_SH1, _SC1, _G1, _SH2, _SC2, _G2 = range(6)


def _mod_spec(which, bb, row0):
    assert row0 % bb == 0
    return pl.BlockSpec((bb, 1, D_MODEL), lambda i, j: (i + row0 // bb, 0, which))


def _trace_skewed(tiles, lead):
    tiles = list(tiles)
    started, live = 0, []
    while started < len(tiles) or live:
        if started < len(tiles) and (not live or live[-1][1] >= lead):
            live.append([tiles[started], 0])
            started += 1
        for entry in list(live):
            try:
                next(entry[0])
                entry[1] += 1
            except StopIteration:
                live.remove(entry)


def _inproj_kernel(x_ref, sc_ref, sh_ref, g_ref, lbl_ref, w_ref,
                   qa_ref, lf_ref, ka_ref, va_ref, gs_ref, qb_ref, kb_ref, vb_ref,
                   gta_ref, gtb_ref, kf_ref, vf_ref, *, n_keep, rows_minor):
    bb, tl, d = x_ref.shape
    j = pl.program_id(1)
    keep = j >= pl.num_programs(1) - n_keep

    lbl = lbl_ref[...]
    e = jnp.exp(lbl - jnp.max(lbl, axis=0, keepdims=True))
    lb = e[0:1] / jnp.sum(e, axis=0, keepdims=True)

    def tile(bs, ts):
        h = _rms(x_ref[bs, ts, :], g_ref[...]) * (1.0 + sc_ref[bs]) + sh_ref[bs]
        hb = h.reshape(-1, d).astype(BF16)
        nb, nt_ = bs.stop - bs.start, ts.stop - ts.start
        yield

        def proj(off, width):
            return jnp.dot(hb, w_ref[:, off:off + width], preferred_element_type=F32)

        def put(ref, val):
            ref[bs, ts, :] = val.reshape(nb, nt_, -1).astype(ref.dtype)

        put(gta_ref, _sigmoid(proj(_OFF_GATE_A, D_MODEL)))
        yield
        put(gtb_ref, _sigmoid(proj(_OFF_GATE_B, D_MODEL)))
        yield
        put(qa_ref, _silu(proj(_OFF_QA, K_A)))
        f = lb + (1.0 - lb) * _sigmoid(proj(_OFF_FA, K_A))
        put(lf_ref, jnp.log(f))
        put(ka_ref, 1.0 - f)
        yield
        put(gs_ref, _silu(proj(_OFF_GA, V_A)))
        put(va_ref, proj(_OFF_IA, V_A))
        yield
        put(qb_ref, proj(_OFF_QB, W_B) * _Q_SCALE)
        kb = proj(_OFF_KB, W_B)
        vb = proj(_OFF_VB, W_B)
        put(kb_ref, kb)
        put(vb_ref, vb)
        cache_rows.append((bs, ts, kb, vb))
        yield

    cache_rows = []
    _trace_skewed([tile(bs, ts) for bs, ts in _subtiles(bb, tl)], lead=1)

    @pl.when(keep)
    def _():
        for bs, ts, kb, vb in cache_rows:
            nb, nt_ = bs.stop - bs.start, ts.stop - ts.start
            for ref, val in ((kf_ref, kb), (vf_ref, vb)):
                if rows_minor:
                    assert nb == 1
                    ref[bs, :, :, ts] = val.T.reshape(1, H_B, DH_B, nt_)
                else:
                    for hd in range(H_B):
                        ref[bs, pl.ds(ts.start * H_B + hd, nt_, stride=H_B), :] = (
                            val[:, hd * DH_B:(hd + 1) * DH_B].reshape(nb, nt_, DH_B))


def _inproj(x, mod, mod_row0, norm_g, lb_logits, w_in_bf16, bb, tl):
    b, l, d = x.shape
    nj = l // tl
    keep_rows = min(WINDOW, l)
    assert keep_rows % tl == 0
    n_keep = keep_rows // tl
    widths = [K_A, K_A, K_A, V_A, V_A, W_B, W_B, W_B, D_MODEL, D_MODEL]
    dtypes = [BF16, F32, BF16, BF16, BF16, BF16, BF16, BF16, BF16, BF16]
    tok = lambda w: pl.BlockSpec((bb, tl, w), lambda i, j: (i, j, 0))
    kept_block = lambda j: jnp.maximum(j - (nj - n_keep), 0)
    rows_minor = keep_rows >= _LANES
    if rows_minor:
        kept = pl.BlockSpec((bb, H_B, DH_B, tl), lambda i, j: (i, 0, 0, kept_block(j)))
        kept_shape = jax.ShapeDtypeStruct((b, H_B, DH_B, keep_rows), F32)
    else:
        kept = pl.BlockSpec((bb, tl * H_B, DH_B), lambda i, j: (i, kept_block(j), 0))
        kept_shape = jax.ShapeDtypeStruct((b, keep_rows * H_B, DH_B), F32)
    return pl.pallas_call(
        functools.partial(_inproj_kernel, n_keep=n_keep, rows_minor=rows_minor),
        out_shape=([jax.ShapeDtypeStruct((b, l, w), dt) for w, dt in zip(widths, dtypes)]
                   + [kept_shape] * 2),
        grid=(b // bb, nj),
        in_specs=[tok(d), _mod_spec(_SC1, bb, mod_row0), _mod_spec(_SH1, bb, mod_row0),
                  _resident((1, d)), _resident(lb_logits.shape), _resident(w_in_bf16.shape)],
        out_specs=[tok(w) for w in widths] + [kept, kept],
        compiler_params=_params("arbitrary", "arbitrary"),
        name="norm_inproj",
    )(x, mod, mod, norm_g, lb_logits, w_in_bf16)


_HGRN_TILE = 256


def _hgrn_kernel(qa_ref, lf_ref, ka_ref, va_ref, gs_ref, og_ref, s0_ref,
                 o_ref, sfin_ref, st_ref):
    j = pl.program_id(1)
    bb, tb, _ = qa_ref.shape
    th = min(tb, _HGRN_TILE)
    nc = th // CHUNK

    @pl.when(j == 0)
    def _():
        for bi in range(bb):
            for h in range(H_A):
                st_ref[bi, h] = s0_ref[bi, h].T

    row = lax.broadcasted_iota(jnp.int32, (th, th), 0)
    col = lax.broadcasted_iota(jnp.int32, (th, th), 1)
    tri = (row >= col) & (row // CHUNK == col // CHUNK)
    tri_bf = jnp.where(tri, 1.0, 0.0).astype(BF16)
    nt = (((1,), (1,)), ((), ()))
    tn = (((0,), (0,)), ((), ()))
    chunked = lambda a: a.reshape(nc, CHUNK, a.shape[-1])
    flat = lambda a: a.reshape(th, a.shape[-1])

    def seq_tile(t0, bi):
        tile = slice(t0, t0 + th)
        cum = chunked(sum(jnp.dot(tri_bf, piece, preferred_element_type=F32)
                          for piece in _split3(lf_ref[bi, tile, :])[:2]))
        mid = cum[:, CHUNK // 2:CHUNK // 2 + 1]
        tot = cum[:, CHUNK - 1:CHUNK]
        q = chunked(qa_ref[bi, tile, :].astype(F32))
        k = chunked(ka_ref[bi, tile, :].astype(F32))
        v = va_ref[bi, tile, :].astype(BF16)
        qe32 = q * jnp.exp(cum - mid)
        ke32 = k * jnp.exp(mid - cum)
        qe = flat(qe32).astype(BF16)
        ke = flat(ke32).astype(BF16)
        qd = flat(qe32 * jnp.exp(mid)).astype(BF16)
        kd = flat(ke32 * jnp.exp(tot - mid)).astype(BF16)
        decay = jnp.exp(tot)
        yield
        for h in range(H_A):
            lanes = slice(h * DK_A, (h + 1) * DK_A)
            a = lax.dot_general(qe[:, lanes], ke[:, lanes], nt, preferred_element_type=F32)
            a = jnp.where(tri, a, 0.0).astype(BF16)
            intra = jnp.dot(a, v[:, lanes], preferred_element_type=F32)
            st = st_ref[bi, h]
            inter = []
            for c in range(nc):
                rows = slice(c * CHUNK, (c + 1) * CHUNK)
                inter.append(lax.dot_general(qd[rows, lanes], st.astype(BF16), nt,
                                             preferred_element_type=F32))
                st = decay[c, :, lanes] * st + lax.dot_general(
                    v[rows, lanes], kd[rows, lanes], tn, preferred_element_type=F32)
            st_ref[bi, h] = st
            o = intra + jnp.concatenate(inter, axis=0)
            o = o * lax.rsqrt(jnp.mean(o * o, axis=-1, keepdims=True) + EPS)
            gated = o * og_ref[:, lanes] * gs_ref[bi, tile, lanes].astype(F32)
            o_ref[bi, tile, lanes] = gated.astype(o_ref.dtype)
            yield

    _trace_skewed([seq_tile(t0, bi) for t0 in range(0, tb, th) for bi in range(bb)], lead=2)

    @pl.when(j == pl.num_programs(1) - 1)
    def _():
        for bi in range(bb):
            for h in range(H_A):
                sfin_ref[bi, h] = st_ref[bi, h].T


def _hgrn(qa, lf, ka, va, gs, out_norm, s0, bb, th):
    b, l, _ = qa.shape
    tok = pl.BlockSpec((bb, th, K_A), lambda i, j: (i, j, 0))
    state = pl.BlockSpec((bb, H_A, DK_A, DV_A), lambda i, j: (i, 0, 0, 0))
    return pl.pallas_call(
        _hgrn_kernel,
        out_shape=[jax.ShapeDtypeStruct((b, l, V_A), BF16),
                   jax.ShapeDtypeStruct((b, H_A, DK_A, DV_A), F32)],
        grid=(b // bb, l // th),
        in_specs=[tok, tok, tok, tok, tok, _resident((1, V_A)), state],
        out_specs=[tok, state],
        scratch_shapes=[pltpu.VMEM((bb, H_A, DV_A, DK_A), F32)],
        compiler_params=_params("arbitrary", "arbitrary"),
        name="hgrn2_recurrence",
    )(qa, lf, ka, va, gs, out_norm, s0)


_HEADS_PER_UNIT = 4
_UNIT_LANES = _HEADS_PER_UNIT * DH_B
_UNIT_ROWS = _HEADS_PER_UNIT * CHUNK


def _attn_kernel(q_ref, kp_ref, kc_ref, vp_ref, vc_ref, bias_ref, o_ref, kbuf, vbuf):
    t = pl.program_id(1)
    bb, tq, _ = q_ref.shape
    for prev, cur, buf in ((kp_ref, kc_ref, kbuf), (vp_ref, vc_ref, vbuf)):
        buf[:, 0:WINDOW, :] = prev[...].astype(BF16)
        buf[:, WINDOW:WINDOW + tq, :] = cur[...].astype(BF16)

    lane_head = lax.broadcasted_iota(jnp.int32, (CHUNK, _UNIT_LANES), 1) // DH_B
    own = [lane_head == h for h in range(_HEADS_PER_UNIT)]
    key = lax.broadcasted_iota(jnp.int32, (_UNIT_ROWS, BAND), 1)
    nt = (((1,), (1,)), ((), ()))

    def tile(masked):
        for bi in range(bb):
            for c in range(tq // CHUNK):
                rows = slice(c * CHUNK, (c + 1) * CHUNK)
                band = slice(c * CHUNK, c * CHUNK + BAND)
                for g in range(H_B // _HEADS_PER_UNIT):
                    lanes = slice(g * _UNIT_LANES, (g + 1) * _UNIT_LANES)
                    q4 = q_ref[bi, rows, lanes]
                    qs = jnp.concatenate([jnp.where(m, q4, jnp.zeros_like(q4)) for m in own],
                                         axis=0).astype(BF16)
                    s = lax.dot_general(qs, kbuf[bi, band, lanes], nt,
                                        preferred_element_type=F32)
                    s = s + bias_ref[g]
                    if masked:
                        s = jnp.where(key + c * CHUNK >= WINDOW, s, NEG)
                    p = jnp.exp2(s - jnp.max(s, axis=-1, keepdims=True))
                    inv = 1.0 / jnp.sum(p, axis=-1, keepdims=True)
                    o4 = jnp.dot(p.astype(BF16), vbuf[bi, band, lanes],
                                 preferred_element_type=F32) * inv
                    o = jnp.where(own[0], o4[0:CHUNK], 0.0)
                    for h in range(1, _HEADS_PER_UNIT):
                        o = jnp.where(own[h], o4[h * CHUNK:(h + 1) * CHUNK], o)
                    o_ref[bi, rows, lanes] = o.astype(o_ref.dtype)

    pl.when(t == 0)(lambda: tile(True))
    pl.when(t > 0)(lambda: tile(False))


def _attn_cache_kernel(q_ref, kt_ref, kc_ref, vt_ref, vc_ref, bias_ref, o_ref):
    bb = q_ref.shape[0]
    lane_head = lax.broadcasted_iota(jnp.int32, (CHUNK, _UNIT_LANES), 1) // DH_B
    own = [lane_head == h for h in range(_HEADS_PER_UNIT)]
    nt = (((1,), (1,)), ((), ()))
    dot = lambda a, b: jnp.dot(a, b, preferred_element_type=F32)
    dot_nt = lambda a, b: lax.dot_general(a, b, nt, preferred_element_type=F32)

    for bi in range(bb):
        for g in range(H_B // _HEADS_PER_UNIT):
            lanes = slice(g * _UNIT_LANES, (g + 1) * _UNIT_LANES)
            heads = slice(g * _HEADS_PER_UNIT, (g + 1) * _HEADS_PER_UNIT)
            kt4 = kt_ref[bi, heads].reshape(_UNIT_LANES, WINDOW).astype(BF16)
            vt4 = vt_ref[bi, heads].reshape(_UNIT_LANES, WINDOW).astype(BF16)
            q4 = q_ref[bi, :, lanes]
            qs = jnp.concatenate([jnp.where(m, q4, jnp.zeros_like(q4)) for m in own],
                                 axis=0).astype(BF16)
            s_old = dot(qs, kt4) + bias_ref[g, :, 0:WINDOW]
            s_new = dot_nt(qs, kc_ref[bi, :, lanes]) + bias_ref[g, :, WINDOW:BAND]
            top = jnp.maximum(jnp.max(s_old, axis=-1, keepdims=True),
                              jnp.max(s_new, axis=-1, keepdims=True))
            p_old = jnp.exp2(s_old - top)
            p_new = jnp.exp2(s_new - top)
            inv = 1.0 / (jnp.sum(p_old, axis=-1, keepdims=True)
                         + jnp.sum(p_new, axis=-1, keepdims=True))
            o4 = (dot_nt(p_old.astype(BF16), vt4)
                  + dot(p_new.astype(BF16), vc_ref[bi, :, lanes])) * inv
            o = jnp.where(own[0], o4[0:CHUNK], 0.0)
            for h in range(1, _HEADS_PER_UNIT):
                o = jnp.where(own[h], o4[h * CHUNK:(h + 1) * CHUNK], o)
            o_ref[bi, :, lanes] = o.astype(o_ref.dtype)


def _attention(q, k, v, cache_kt, cache_vt, bias, bb, tq):
    b, l, _ = q.shape
    cur = pl.BlockSpec((bb, tq, W_B), lambda i, j: (i, j, 0))
    bias = bias.reshape(H_B // _HEADS_PER_UNIT, _UNIT_ROWS, BAND)
    out_shape = jax.ShapeDtypeStruct((b, l, W_B), BF16)
    if cache_kt is None:
        assert tq % WINDOW == 0
        per_tile = tq // WINDOW
        prev = pl.BlockSpec((bb, WINDOW, W_B),
                            lambda i, j: (i, jnp.maximum(j * per_tile - 1, 0), 0))
        return pl.pallas_call(
            _attn_kernel,
            out_shape=out_shape,
            grid=(b // bb, l // tq),
            in_specs=[cur, prev, cur, prev, cur, _resident(bias.shape)],
            out_specs=cur,
            scratch_shapes=[pltpu.VMEM((bb, WINDOW + tq, W_B), BF16),
                            pltpu.VMEM((bb, WINDOW + tq, W_B), BF16)],
            compiler_params=_params("arbitrary", "arbitrary"),
            name="band_attention",
        )(q, k, k, v, v, bias)
    assert tq == l == CHUNK and cache_kt.shape[1:] == (H_B, DH_B, WINDOW)
    cached = pl.BlockSpec((bb, H_B, DH_B, WINDOW), lambda i, j: (i, 0, 0, 0))
    return pl.pallas_call(
        _attn_cache_kernel,
        out_shape=out_shape,
        grid=(b // bb, 1),
        in_specs=[cur, cached, cur, cached, cur, _resident(bias.shape)],
        out_specs=cur,
        compiler_params=_params("arbitrary", "arbitrary"),
        name="cache_attention",
    )(q, cache_kt, k, cache_vt, v, bias)


def _post_kernel(x_ref, oa_ref, ob_ref, gta_ref, gtb_ref, g1_ref, sc2_ref, sh2_ref, g2_ref,
                 nffn_ref, nfin_ref, wa_ref, wb_ref, wo_ref, wfi_ref, wfo_ref, y_ref):
    bb, tl, d = x_ref.shape

    def tile(bs, ts):
        x = x_ref[bs, ts, :]
        n = x.shape[0] * x.shape[1]
        flat = lambda ref: ref[bs, ts, :].reshape(n, ref.shape[-1])
        per_tok = lambda v: v.reshape(x.shape)

        ya = jnp.dot(flat(oa_ref), wa_ref[...], preferred_element_type=F32)
        yb = jnp.dot(flat(ob_ref), wb_ref[...], preferred_element_type=F32)
        merged = flat(gta_ref).astype(F32) * ya + flat(gtb_ref).astype(F32) * yb
        yield
        y = jnp.dot(merged.astype(BF16), wo_ref[...], preferred_element_type=F32)
        x1 = x + g1_ref[bs] * per_tok(y)
        h2 = _rms(x1, nffn_ref[...]) * (1.0 + sc2_ref[bs]) + sh2_ref[bs]
        hb = h2.reshape(n, d).astype(BF16)
        yield
        acc = jnp.zeros((n, d), F32)
        for lo, hi in _FFN_CHUNKS:
            a = jnp.dot(hb, wfi_ref[:, lo:hi], preferred_element_type=F32)
            u = jnp.dot(hb, wfi_ref[:, D_FF + lo:D_FF + hi], preferred_element_type=F32)
            acc = acc + jnp.dot((_silu(a) * u).astype(BF16), wfo_ref[lo:hi, :],
                                preferred_element_type=F32)
            yield
        x2 = x1 + g2_ref[bs] * per_tok(acc)
        y_ref[bs, ts, :] = _rms(x2, nfin_ref[...])
        yield

    _trace_skewed([tile(bs, ts) for bs, ts in _subtiles(bb, tl)], lead=1)


def _post(x, oa, ob, gta, gtb, mod, mod_row0, norm_ffn, norm_final,
          wa, wb, wo, wfi, wfo, bb, tl):
    b, l, d = x.shape
    tok = lambda w: pl.BlockSpec((bb, tl, w), lambda i, j: (i, j, 0))
    per_b = lambda which: _mod_spec(which, bb, mod_row0)
    return pl.pallas_call(
        _post_kernel,
        out_shape=jax.ShapeDtypeStruct((b, l, d), F32),
        grid=(b // bb, l // tl),
        in_specs=[tok(d), tok(V_A), tok(W_B), tok(d), tok(d),
                  per_b(_G1), per_b(_SC2), per_b(_SH2), per_b(_G2),
                  _resident((1, d)), _resident((1, d)), _resident(wa.shape),
                  _resident(wb.shape), _resident(wo.shape), _resident(wfi.shape),
                  _resident(wfo.shape)],
        out_specs=tok(d),
        compiler_params=_params("arbitrary", "arbitrary"),
        name="merge_out_ffn",
    )(x, oa, ob, gta, gtb, mod, mod, mod, mod, norm_ffn, norm_final, wa, wb, wo, wfi, wfo)


def _trunk(x, mod, mod_row0, s0, cache_k, cache_v, bias, w, *, in_tile, post_tile, hgrn_tile,
           attn_tile):
    b, l, d = x.shape

    qa, lf, ka, va, gs, qb, kb, vb, gta, gtb, k_rows, v_rows = _inproj(
        x, mod, mod_row0, w["norm_mix"], w["lb_logits"], w["w_in"], *in_tile)

    oa, s_fin = _hgrn(qa, lf, ka, va, gs, w["out_norm"], s0, *hgrn_tile)
    ob = _attention(qb, kb, vb, cache_k, cache_v, bias, *attn_tile)

    y = _post(x, oa, ob, gta, gtb, mod, mod_row0, w["norm_ffn"], w["norm_final"],
              w["w_branch_a"], w["w_branch_b"], w["w_out"], w["w_ffn_in"], w["w_ffn_out"],
              *post_tile)
    def rows_first(r):
        if r.ndim == 4:
            return r.transpose(0, 3, 1, 2)[None]
        return r.reshape(1, b, r.shape[1] // H_B, H_B, DH_B)

    return y, s_fin[None], rows_first(k_rows), rows_first(v_rows)


def kernel(x_prompt, x_sample, c_prompt, c_sample, state_hgrn, cache_k, cache_v, w_ada, b_ada,
           norm_mix, w_in, hgrn_lb_logits, hgrn_out_norm, w_branch_a, rel_bias, w_branch_b,
           w_out, norm_ffn, w_ffn_in, w_ffn_out, norm_final):
    assert w_ada.shape[0] == 1, "single-layer trunk"
    w = dict(
        w_ada=w_ada[0], b_ada=b_ada[0], norm_mix=norm_mix[0].reshape(1, D_MODEL),
        lb_logits=hgrn_lb_logits, w_in=w_in[0].astype(BF16),
        out_norm=hgrn_out_norm[0].reshape(1, V_A),
        w_branch_a=w_branch_a[0].astype(BF16), w_branch_b=w_branch_b[0].astype(BF16),
        w_out=w_out[0].astype(BF16), norm_ffn=norm_ffn[0].reshape(1, D_MODEL),
        w_ffn_in=w_ffn_in[0].astype(BF16), w_ffn_out=w_ffn_out[0].astype(BF16),
        norm_final=norm_final.reshape(1, D_MODEL),
    )
    bias = _bias_tile(rel_bias[0])

    bp = x_prompt.shape[0]
    bs, ls = x_sample.shape[:2]
    mod = _modulation(jnp.concatenate([c_sample, c_prompt], axis=0), w["w_ada"], w["b_ada"])
    mod = mod.reshape(bs + bp, 1, 6 * D_MODEL)

    s0_prompt = jnp.zeros((bp, H_A, DK_A, DV_A), F32)
    y_p, s_p, k_p, v_p = _trunk(x_prompt, mod, bs, s0_prompt, None, None, bias, w,
                                in_tile=(1, 512), post_tile=(1, 512), hgrn_tile=(bp, 1024),
                                attn_tile=(1, 2 * WINDOW))
    y_s, s_s, k_s, v_s = _trunk(x_sample, mod, 0, state_hgrn[0],
                                cache_k[0].transpose(0, 2, 3, 1), cache_v[0].transpose(0, 2, 3, 1),
                                bias, w, in_tile=(8, ls), post_tile=(8, ls), hgrn_tile=(8, ls),
                                attn_tile=(4, ls))
    return (y_p, y_s, s_p, k_p, v_p, s_s, k_s, v_s)
```

```python
import functools

import jax
import jax.numpy as jnp
from jax import lax
from jax.experimental import pallas as pl
from jax.experimental.pallas import tpu as pltpu

D_MODEL = 1024
CHUNK = 64
H_A, DK_A, DV_A = 4, 128, 128
K_A = H_A * DK_A
V_A = H_A * DV_A
H_B, DH_B = 8, 64
W_B = H_B * DH_B
N_PAST_CHUNKS = 8
WINDOW = N_PAST_CHUNKS * CHUNK
BAND = WINDOW + CHUNK
REL_CLIP = 128
NUM_REL = CHUNK + REL_CLIP
D_FF = 2816
EPS = 1e-6
NEG = -1e30
_LOG2E = 1.4426950408889634
_Q_SCALE = (DH_B ** -0.5) * _LOG2E

F32 = jnp.float32
BF16 = jnp.bfloat16

_OFF_QA, _OFF_FA, _OFF_IA, _OFF_GA = 0, K_A, 2 * K_A, 2 * K_A + V_A
_OFF_QB = 2 * K_A + 2 * V_A
_OFF_KB = _OFF_QB + W_B
_OFF_VB = _OFF_KB + W_B
_OFF_GATE_A = _OFF_VB + W_B
_OFF_GATE_B = _OFF_GATE_A + D_MODEL

_VMEM_LIMIT_BYTES = 56 * 1024 * 1024
_LANES = 128
_MXU_DIM = 256
_FFN_CHUNKS = ((0, 6 * _MXU_DIM), (6 * _MXU_DIM, D_FF))
assert D_FF % _MXU_DIM == 0


def _params(*sem):
    return pltpu.CompilerParams(dimension_semantics=sem, vmem_limit_bytes=_VMEM_LIMIT_BYTES)


def _resident(shape):
    zeros = (0,) * len(shape)
    return pl.BlockSpec(shape, lambda *_: zeros, pipeline_mode=pl.Buffered(1))


def _sigmoid(x):
    return 0.5 * jnp.tanh(0.5 * x) + 0.5


def _silu(x):
    return x * _sigmoid(x)


def _rms(x, g):
    return x * lax.rsqrt(jnp.mean(x * x, axis=-1, keepdims=True) + EPS) * g


def _split3(x):
    hi = x.astype(BF16)
    r = x - hi.astype(F32)
    mid = r.astype(BF16)
    lo = (r - mid.astype(F32)).astype(BF16)
    return hi, mid, lo


def _mod_kernel(c_ref, w_ref, b_ref, o_ref):
    s_hi, s_lo, _ = _split3(_silu(c_ref[...]))
    w_hi, w_lo, _ = _split3(w_ref[...])
    dot = lambda a, b: jnp.dot(a, b, preferred_element_type=F32)
    o_ref[...] = dot(s_hi, w_hi) + (dot(s_lo, w_hi) + dot(s_hi, w_lo)) + b_ref[...]


def _modulation(c, w_ada, b_ada):
    b = c.shape[0]
    n_out = w_ada.shape[1]
    return pl.pallas_call(
        _mod_kernel,
        out_shape=jax.ShapeDtypeStruct((b, n_out), F32),
        grid=(n_out // D_MODEL,),
        in_specs=[
            pl.BlockSpec((b, D_MODEL), lambda j: (0, 0)),
            pl.BlockSpec((D_MODEL, D_MODEL), lambda j: (0, j)),
            pl.BlockSpec((1, D_MODEL), lambda j: (0, j)),
        ],
        out_specs=pl.BlockSpec((b, D_MODEL), lambda j: (0, j)),
        compiler_params=_params("arbitrary"),
        name="adaln_mod",
    )(c, w_ada, b_ada.reshape(1, n_out))


_BIAS_EXT = 640


def _bias_kernel(rb_ref, o_ref):
    m = lax.broadcasted_iota(jnp.int32, (NUM_REL, _BIAS_EXT), 1)
    slot = lax.broadcasted_iota(jnp.int32, (NUM_REL, _BIAS_EXT), 0)
    idx = jnp.clip(WINDOW + CHUNK - 1 - m, -(CHUNK - 1), REL_CLIP) + (CHUNK - 1)
    onehot = jnp.where(slot == idx, 1.0, 0.0).astype(BF16)
    ext = sum(jnp.dot(piece, onehot, preferred_element_type=F32)
              for piece in _split3(rb_ref[...]))
    ext = ext * _LOG2E
    for i in range(CHUNK):
        o_ref[i] = ext[:, CHUNK - 1 - i:CHUNK - 1 - i + BAND]


def _bias_tile(rel_bias):
    out = pl.pallas_call(
        _bias_kernel,
        out_shape=jax.ShapeDtypeStruct((CHUNK, H_B, BAND), F32),
        grid=(1,),
        in_specs=[pl.BlockSpec((H_B, NUM_REL), lambda i: (0, 0))],
        out_specs=pl.BlockSpec((CHUNK, H_B, BAND), lambda i: (0, 0, 0)),
        compiler_params=_params("arbitrary"),
        name="rel_bias_tile",
    )(rel_bias)
    return out.transpose(1, 0, 2)


_SUB_ROWS = 256


def _subtiles(bb, tl):
    if bb == 1:
        step = min(tl, _SUB_ROWS)
        return [(slice(0, 1), slice(t, t + step)) for t in range(0, tl, step)]
    step = max(1, min(bb, _SUB_ROWS // tl))
    return [(slice(i, i + step), slice(0, tl)) for i in range(0, bb, step)]


_SH1, _SC1, _G1, _SH2, _SC2, _G2 = range(6)


def _mod_spec(which, bb, row0):
    assert row0 % bb == 0
    return pl.BlockSpec((bb, 1, D_MODEL), lambda i, j: (i + row0 // bb, 0, which))


def _trace_skewed(tiles, lead):
    tiles = list(tiles)
    started, live = 0, []
    while started < len(tiles) or live:
        if started < len(tiles) and (not live or live[-1][1] >= lead):
            live.append([tiles[started], 0])
            started += 1
        for entry in list(live):
            try:
                next(entry[0])
                entry[1] += 1
            except StopIteration:
                live.remove(entry)


def _inproj_kernel(x_ref, sc_ref, sh_ref, g_ref, lbl_ref, w_ref,
                   qa_ref, lf_ref, ka_ref, va_ref, gs_ref, qb_ref, kb_ref, vb_ref,
                   gta_ref, gtb_ref, kf_ref, vf_ref, *, n_keep, rows_minor):
    bb, tl, d = x_ref.shape
    j = pl.program_id(1)
    keep = j >= pl.num_programs(1) - n_keep

    lbl = lbl_ref[...]
    e = jnp.exp(lbl - jnp.max(lbl, axis=0, keepdims=True))
    lb = e[0:1] / jnp.sum(e, axis=0, keepdims=True)

    def tile(bs, ts):
        h = _rms(x_ref[bs, ts, :], g_ref[...]) * (1.0 + sc_ref[bs]) + sh_ref[bs]
        hb = h.reshape(-1, d).astype(BF16)
        nb, nt_ = bs.stop - bs.start, ts.stop - ts.start
        yield

        def proj(off, width):
            return jnp.dot(hb, w_ref[:, off:off + width], preferred_element_type=F32)

        def put(ref, val):
            ref[bs, ts, :] = val.reshape(nb, nt_, -1).astype(ref.dtype)

        put(gta_ref, _sigmoid(proj(_OFF_GATE_A, D_MODEL)))
        yield
        put(gtb_ref, _sigmoid(proj(_OFF_GATE_B, D_MODEL)))
        yield
        put(qa_ref, _silu(proj(_OFF_QA, K_A)))
        f = lb + (1.0 - lb) * _sigmoid(proj(_OFF_FA, K_A))
        put(lf_ref, jnp.log(f))
        put(ka_ref, 1.0 - f)
        yield
        put(gs_ref, _silu(proj(_OFF_GA, V_A)))
        put(va_ref, proj(_OFF_IA, V_A))
        yield
        put(qb_ref, proj(_OFF_QB, W_B) * _Q_SCALE)
        kb = proj(_OFF_KB, W_B)
        vb = proj(_OFF_VB, W_B)
        put(kb_ref, kb)
        put(vb_ref, vb)
        cache_rows.append((bs, ts, kb, vb))
        yield

    cache_rows = []
    _trace_skewed([tile(bs, ts) for bs, ts in _subtiles(bb, tl)], lead=1)

    @pl.when(keep)
    def _():
        for bs, ts, kb, vb in cache_rows:
            nb, nt_ = bs.stop - bs.start, ts.stop - ts.start
            for ref, val in ((kf_ref, kb), (vf_ref, vb)):
                if rows_minor:
                    assert nb == 1
                    ref[bs, :, :, ts] = val.T.reshape(1, H_B, DH_B, nt_)
                else:
                    for hd in range(H_B):
                        ref[bs, pl.ds(ts.start * H_B + hd, nt_, stride=H_B), :] = (
                            val[:, hd * DH_B:(hd + 1) * DH_B].reshape(nb, nt_, DH_B))


def _inproj(x, mod, mod_row0, norm_g, lb_logits, w_in_bf16, bb, tl):
    b, l, d = x.shape
    nj = l // tl
    keep_rows = min(WINDOW, l)
    assert keep_rows % tl == 0
    n_keep = keep_rows // tl
    widths = [K_A, K_A, K_A, V_A, V_A, W_B, W_B, W_B, D_MODEL, D_MODEL]
    dtypes = [BF16, F32, BF16, BF16, BF16, BF16, BF16, BF16, BF16, BF16]
    tok = lambda w: pl.BlockSpec((bb, tl, w), lambda i, j: (i, j, 0))
    kept_block = lambda j: jnp.maximum(j - (nj - n_keep), 0)
    rows_minor = keep_rows >= _LANES
    if rows_minor:
        kept = pl.BlockSpec((bb, H_B, DH_B, tl), lambda i, j: (i, 0, 0, kept_block(j)))
        kept_shape = jax.ShapeDtypeStruct((b, H_B, DH_B, keep_rows), F32)
    else:
        kept = pl.BlockSpec((bb, tl * H_B, DH_B), lambda i, j: (i, kept_block(j), 0))
        kept_shape = jax.ShapeDtypeStruct((b, keep_rows * H_B, DH_B), F32)
    return pl.pallas_call(
        functools.partial(_inproj_kernel, n_keep=n_keep, rows_minor=rows_minor),
        out_shape=([jax.ShapeDtypeStruct((b, l, w), dt) for w, dt in zip(widths, dtypes)]
                   + [kept_shape] * 2),
        grid=(b // bb, nj),
        in_specs=[tok(d), _mod_spec(_SC1, bb, mod_row0), _mod_spec(_SH1, bb, mod_row0),
                  _resident((1, d)), _resident(lb_logits.shape), _resident(w_in_bf16.shape)],
        out_specs=[tok(w) for w in widths] + [kept, kept],
        compiler_params=_params("arbitrary", "arbitrary"),
        name="norm_inproj",
    )(x, mod, mod, norm_g, lb_logits, w_in_bf16)


_HGRN_TILE = 256


def _hgrn_kernel(qa_ref, lf_ref, ka_ref, va_ref, gs_ref, og_ref, s0_ref,
                 o_ref, sfin_ref, st_ref):
    j = pl.program_id(1)
    bb, tb, _ = qa_ref.shape
    th = min(tb, _HGRN_TILE)
    nc = th // CHUNK

    @pl.when(j == 0)
    def _():
        for bi in range(bb):
            for h in range(H_A):
                st_ref[bi, h] = s0_ref[bi, h].T

    row = lax.broadcasted_iota(jnp.int32, (th, th), 0)
    col = lax.broadcasted_iota(jnp.int32, (th, th), 1)
    tri = (row >= col) & (row // CHUNK == col // CHUNK)
    tri_bf = jnp.where(tri, 1.0, 0.0).astype(BF16)
    nt = (((1,), (1,)), ((), ()))
    tn = (((0,), (0,)), ((), ()))
    chunked = lambda a: a.reshape(nc, CHUNK, a.shape[-1])
    flat = lambda a: a.reshape(th, a.shape[-1])

    def seq_tile(t0, bi):
        tile = slice(t0, t0 + th)
        cum = chunked(sum(jnp.dot(tri_bf, piece, preferred_element_type=F32)
                          for piece in _split3(lf_ref[bi, tile, :])[:2]))
        mid = cum[:, CHUNK // 2:CHUNK // 2 + 1]
        tot = cum[:, CHUNK - 1:CHUNK]
        q = chunked(qa_ref[bi, tile, :].astype(F32))
        k = chunked(ka_ref[bi, tile, :].astype(F32))
        v = va_ref[bi, tile, :].astype(BF16)
        qe32 = q * jnp.exp(cum - mid)
        ke32 = k * jnp.exp(mid - cum)
        qe = flat(qe32).astype(BF16)
        ke = flat(ke32).astype(BF16)
        qd = flat(qe32 * jnp.exp(mid)).astype(BF16)
        kd = flat(ke32 * jnp.exp(tot - mid)).astype(BF16)
        decay = jnp.exp(tot)
        yield
        for h in range(H_A):
            lanes = slice(h * DK_A, (h + 1) * DK_A)
            a = lax.dot_general(qe[:, lanes], ke[:, lanes], nt, preferred_element_type=F32)
            a = jnp.where(tri, a, 0.0).astype(BF16)
            intra = jnp.dot(a, v[:, lanes], preferred_element_type=F32)
            st = st_ref[bi, h]
            inter = []
            for c in range(nc):
                rows = slice(c * CHUNK, (c + 1) * CHUNK)
                inter.append(lax.dot_general(qd[rows, lanes], st.astype(BF16), nt,
                                             preferred_element_type=F32))
                st = decay[c, :, lanes] * st + lax.dot_general(
                    v[rows, lanes], kd[rows, lanes], tn, preferred_element_type=F32)
            st_ref[bi, h] = st
            o = intra + jnp.concatenate(inter, axis=0)
            o = o * lax.rsqrt(jnp.mean(o * o, axis=-1, keepdims=True) + EPS)
            gated = o * og_ref[:, lanes] * gs_ref[bi, tile, lanes].astype(F32)
            o_ref[bi, tile, lanes] = gated.astype(o_ref.dtype)
            yield

    _trace_skewed([seq_tile(t0, bi) for t0 in range(0, tb, th) for bi in range(bb)], lead=2)

    @pl.when(j == pl.num_programs(1) - 1)
    def _():
        for bi in range(bb):
            for h in range(H_A):
                sfin_ref[bi, h] = st_ref[bi, h].T


def _hgrn(qa, lf, ka, va, gs, out_norm, s0, bb, th):
    b, l, _ = qa.shape
    tok = pl.BlockSpec((bb, th, K_A), lambda i, j: (i, j, 0))
    state = pl.BlockSpec((bb, H_A, DK_A, DV_A), lambda i, j: (i, 0, 0, 0))
    return pl.pallas_call(
        _hgrn_kernel,
        out_shape=[jax.ShapeDtypeStruct((b, l, V_A), BF16),
                   jax.ShapeDtypeStruct((b, H_A, DK_A, DV_A), F32)],
        grid=(b // bb, l // th),
        in_specs=[tok, tok, tok, tok, tok, _resident((1, V_A)), state],
        out_specs=[tok, state],
        scratch_shapes=[pltpu.VMEM((bb, H_A, DV_A, DK_A), F32)],
        compiler_params=_params("arbitrary", "arbitrary"),
        name="hgrn2_recurrence",
    )(qa, lf, ka, va, gs, out_norm, s0)


_HEADS_PER_UNIT = 4
_UNIT_LANES = _HEADS_PER_UNIT * DH_B
_UNIT_ROWS = _HEADS_PER_UNIT * CHUNK


def _attn_kernel(q_ref, kp_ref, kc_ref, vp_ref, vc_ref, bias_ref, o_ref, kbuf, vbuf):
    t = pl.program_id(1)
    bb, tq, _ = q_ref.shape
    for prev, cur, buf in ((kp_ref, kc_ref, kbuf), (vp_ref, vc_ref, vbuf)):
        buf[:, 0:WINDOW, :] = prev[...].astype(BF16)
        buf[:, WINDOW:WINDOW + tq, :] = cur[...].astype(BF16)

    lane_head = lax.broadcasted_iota(jnp.int32, (CHUNK, _UNIT_LANES), 1) // DH_B
    own = [lane_head == h for h in range(_HEADS_PER_UNIT)]
    key = lax.broadcasted_iota(jnp.int32, (_UNIT_ROWS, BAND), 1)
    nt = (((1,), (1,)), ((), ()))

    def tile(masked):
        def chunk(c, carry):
            r0 = pl.multiple_of(c * CHUNK, CHUNK)
            for bi in range(bb):
                for g in range(H_B // _HEADS_PER_UNIT):
                    lanes = slice(g * _UNIT_LANES, (g + 1) * _UNIT_LANES)
                    q4 = q_ref[bi, pl.ds(r0, CHUNK), lanes]
                    qs = jnp.concatenate([jnp.where(m, q4, jnp.zeros_like(q4)) for m in own],
                                         axis=0).astype(BF16)
                    s = lax.dot_general(qs, kbuf[bi, pl.ds(r0, BAND), lanes], nt,
                                        preferred_element_type=F32)
                    s = s + bias_ref[g]
                    if masked:
                        s = jnp.where(key + r0 >= WINDOW, s, NEG)
                    p = jnp.exp2(s - jnp.max(s, axis=-1, keepdims=True))
                    inv = 1.0 / jnp.sum(p, axis=-1, keepdims=True)
                    o4 = jnp.dot(p.astype(BF16), vbuf[bi, pl.ds(r0, BAND), lanes],
                                 preferred_element_type=F32) * inv
                    o = jnp.where(own[0], o4[0:CHUNK], 0.0)
                    for h in range(1, _HEADS_PER_UNIT):
                        o = jnp.where(own[h], o4[h * CHUNK:(h + 1) * CHUNK], o)
                    o_ref[bi, pl.ds(r0, CHUNK), lanes] = o.astype(o_ref.dtype)
            return carry

        lax.fori_loop(0, tq // CHUNK, chunk, 0, unroll=4)

    pl.when(t == 0)(lambda: tile(True))
    pl.when(t > 0)(lambda: tile(False))


def _attn_cache_kernel(q_ref, kt_ref, kc_ref, vt_ref, vc_ref, bias_ref, o_ref):
    bb = q_ref.shape[0]
    lane_head = lax.broadcasted_iota(jnp.int32, (CHUNK, _UNIT_LANES), 1) // DH_B
    own = [lane_head == h for h in range(_HEADS_PER_UNIT)]
    nt = (((1,), (1,)), ((), ()))
    dot = lambda a, b: jnp.dot(a, b, preferred_element_type=F32)
    dot_nt = lambda a, b: lax.dot_general(a, b, nt, preferred_element_type=F32)

    for bi in range(bb):
        for g in range(H_B // _HEADS_PER_UNIT):
            lanes = slice(g * _UNIT_LANES, (g + 1) * _UNIT_LANES)
            heads = slice(g * _HEADS_PER_UNIT, (g + 1) * _HEADS_PER_UNIT)
            kt4 = kt_ref[bi, heads].reshape(_UNIT_LANES, WINDOW).astype(BF16)
            vt4 = vt_ref[bi, heads].reshape(_UNIT_LANES, WINDOW).astype(BF16)
            q4 = q_ref[bi, :, lanes]
            qs = jnp.concatenate([jnp.where(m, q4, jnp.zeros_like(q4)) for m in own],
                                 axis=0).astype(BF16)
            s_old = dot(qs, kt4) + bias_ref[g, :, 0:WINDOW]
            s_new = dot_nt(qs, kc_ref[bi, :, lanes]) + bias_ref[g, :, WINDOW:BAND]
            top = jnp.maximum(jnp.max(s_old, axis=-1, keepdims=True),
                              jnp.max(s_new, axis=-1, keepdims=True))
            p_old = jnp.exp2(s_old - top)
            p_new = jnp.exp2(s_new - top)
            inv = 1.0 / (jnp.sum(p_old, axis=-1, keepdims=True)
                         + jnp.sum(p_new, axis=-1, keepdims=True))
            o4 = (dot_nt(p_old.astype(BF16), vt4)
                  + dot(p_new.astype(BF16), vc_ref[bi, :, lanes])) * inv
            o = jnp.where(own[0], o4[0:CHUNK], 0.0)
            for h in range(1, _HEADS_PER_UNIT):
                o = jnp.where(own[h], o4[h * CHUNK:(h + 1) * CHUNK], o)
            o_ref[bi, :, lanes] = o.astype(o_ref.dtype)


def _attention(q, k, v, cache_kt, cache_vt, bias, bb, tq):
    b, l, _ = q.shape
    cur = pl.BlockSpec((bb, tq, W_B), lambda i, j: (i, j, 0))
    bias = bias.reshape(H_B // _HEADS_PER_UNIT, _UNIT_ROWS, BAND)
    out_shape = jax.ShapeDtypeStruct((b, l, W_B), BF16)
    if cache_kt is None:
        assert tq % WINDOW == 0
        per_tile = tq // WINDOW
        prev = pl.BlockSpec((bb, WINDOW, W_B),
                            lambda i, j: (i, jnp.maximum(j * per_tile - 1, 0), 0))
        return pl.pallas_call(
            _attn_kernel,
            out_shape=out_shape,
            grid=(b // bb, l // tq),
            in_specs=[cur, prev, cur, prev, cur, _resident(bias.shape)],
            out_specs=cur,
            scratch_shapes=[pltpu.VMEM((bb, WINDOW + tq, W_B), BF16),
                            pltpu.VMEM((bb, WINDOW + tq, W_B), BF16)],
            compiler_params=_params("arbitrary", "arbitrary"),
            name="band_attention",
        )(q, k, k, v, v, bias)
    assert tq == l == CHUNK and cache_kt.shape[1:] == (H_B, DH_B, WINDOW)
    cached = pl.BlockSpec((bb, H_B, DH_B, WINDOW), lambda i, j: (i, 0, 0, 0))
    return pl.pallas_call(
        _attn_cache_kernel,
        out_shape=out_shape,
        grid=(b // bb, 1),
        in_specs=[cur, cached, cur, cached, cur, _resident(bias.shape)],
        out_specs=cur,
        compiler_params=_params("arbitrary", "arbitrary"),
        name="cache_attention",
    )(q, cache_kt, k, cache_vt, v, bias)


def _post_kernel(x_ref, oa_ref, ob_ref, gta_ref, gtb_ref, g1_ref, sc2_ref, sh2_ref, g2_ref,
                 nffn_ref, nfin_ref, wa_ref, wb_ref, wo_ref, wfi_ref, wfo_ref, y_ref):
    bb, tl, d = x_ref.shape

    def tile(bs, ts):
        x = x_ref[bs, ts, :]
        n = x.shape[0] * x.shape[1]
        flat = lambda ref: ref[bs, ts, :].reshape(n, ref.shape[-1])
        per_tok = lambda v: v.reshape(x.shape)

        ya = jnp.dot(flat(oa_ref), wa_ref[...], preferred_element_type=F32)
        yb = jnp.dot(flat(ob_ref), wb_ref[...], preferred_element_type=F32)
        merged = flat(gta_ref).astype(F32) * ya + flat(gtb_ref).astype(F32) * yb
        yield
        y = jnp.dot(merged.astype(BF16), wo_ref[...], preferred_element_type=F32)
        x1 = x + g1_ref[bs] * per_tok(y)
        h2 = _rms(x1, nffn_ref[...]) * (1.0 + sc2_ref[bs]) + sh2_ref[bs]
        hb = h2.reshape(n, d).astype(BF16)
        yield
        acc = jnp.zeros((n, d), F32)
        for lo, hi in _FFN_CHUNKS:
            a = jnp.dot(hb, wfi_ref[:, lo:hi], preferred_element_type=F32)
            u = jnp.dot(hb, wfi_ref[:, D_FF + lo:D_FF + hi], preferred_element_type=F32)
            acc = acc + jnp.dot((_silu(a) * u).astype(BF16), wfo_ref[lo:hi, :],
                                preferred_element_type=F32)
            yield
        x2 = x1 + g2_ref[bs] * per_tok(acc)
        y_ref[bs, ts, :] = _rms(x2, nfin_ref[...])
        yield

    _trace_skewed([tile(bs, ts) for bs, ts in _subtiles(bb, tl)], lead=1)


def _post(x, oa, ob, gta, gtb, mod, mod_row0, norm_ffn, norm_final,
          wa, wb, wo, wfi, wfo, bb, tl):
    b, l, d = x.shape
    tok = lambda w: pl.BlockSpec((bb, tl, w), lambda i, j: (i, j, 0))
    per_b = lambda which: _mod_spec(which, bb, mod_row0)
    return pl.pallas_call(
        _post_kernel,
        out_shape=jax.ShapeDtypeStruct((b, l, d), F32),
        grid=(b // bb, l // tl),
        in_specs=[tok(d), tok(V_A), tok(W_B), tok(d), tok(d),
                  per_b(_G1), per_b(_SC2), per_b(_SH2), per_b(_G2),
                  _resident((1, d)), _resident((1, d)), _resident(wa.shape),
                  _resident(wb.shape), _resident(wo.shape), _resident(wfi.shape),
                  _resident(wfo.shape)],
        out_specs=tok(d),
        compiler_params=_params("arbitrary", "arbitrary"),
        name="merge_out_ffn",
    )(x, oa, ob, gta, gtb, mod, mod, mod, mod, norm_ffn, norm_final, wa, wb, wo, wfi, wfo)


def _trunk(x, mod, mod_row0, s0, cache_k, cache_v, bias, w, *, in_tile, post_tile, hgrn_tile,
           attn_tile):
    b, l, d = x.shape

    qa, lf, ka, va, gs, qb, kb, vb, gta, gtb, k_rows, v_rows = _inproj(
        x, mod, mod_row0, w["norm_mix"], w["lb_logits"], w["w_in"], *in_tile)

    oa, s_fin = _hgrn(qa, lf, ka, va, gs, w["out_norm"], s0, *hgrn_tile)
    ob = _attention(qb, kb, vb, cache_k, cache_v, bias, *attn_tile)

    y = _post(x, oa, ob, gta, gtb, mod, mod_row0, w["norm_ffn"], w["norm_final"],
              w["w_branch_a"], w["w_branch_b"], w["w_out"], w["w_ffn_in"], w["w_ffn_out"],
              *post_tile)
    def rows_first(r):
        if r.ndim == 4:
            return r.transpose(0, 3, 1, 2)[None]
        return r.reshape(1, b, r.shape[1] // H_B, H_B, DH_B)

    return y, s_fin[None], rows_first(k_rows), rows_first(v_rows)


def kernel(x_prompt, x_sample, c_prompt, c_sample, state_hgrn, cache_k, cache_v, w_ada, b_ada,
           norm_mix, w_in, hgrn_lb_logits, hgrn_out_norm, w_branch_a, rel_bias, w_branch_b,
           w_out, norm_ffn, w_ffn_in, w_ffn_out, norm_final):
    assert w_ada.shape[0] == 1, "single-layer trunk"
    w = dict(
        w_ada=w_ada[0], b_ada=b_ada[0], norm_mix=norm_mix[0].reshape(1, D_MODEL),
        lb_logits=hgrn_lb_logits, w_in=w_in[0].astype(BF16),
        out_norm=hgrn_out_norm[0].reshape(1, V_A),
        w_branch_a=w_branch_a[0].astype(BF16), w_branch_b=w_branch_b[0].astype(BF16),
        w_out=w_out[0].astype(BF16), norm_ffn=norm_ffn[0].reshape(1, D_MODEL),
        w_ffn_in=w_ffn_in[0].astype(BF16), w_ffn_out=w_ffn_out[0].astype(BF16),
        norm_final=norm_final.reshape(1, D_MODEL),
    )
    bias = _bias_tile(rel_bias[0])

    bp = x_prompt.shape[0]
    bs, ls = x_sample.shape[:2]
    mod = _modulation(jnp.concatenate([c_sample, c_prompt], axis=0), w["w_ada"], w["b_ada"])
    mod = mod.reshape(bs + bp, 1, 6 * D_MODEL)

    s0_prompt = jnp.zeros((bp, H_A, DK_A, DV_A), F32)
    y_p, s_p, k_p, v_p = _trunk(x_prompt, mod, bs, s0_prompt, None, None, bias, w,
                                in_tile=(1, 512), post_tile=(1, 512), hgrn_tile=(bp, 1024),
                                attn_tile=(1, 2 * WINDOW))
    y_s, s_s, k_s, v_s = _trunk(x_sample, mod, 0, state_hgrn[0],
                                cache_k[0].transpose(0, 2, 3, 1), cache_v[0].transpose(0, 2, 3, 1),
                                bias, w, in_tile=(8, ls), post_tile=(8, ls), hgrn_tile=(8, ls),
                                attn_tile=(4, ls))
    return (y_p, y_s, s_p, k_p, v_p, s_s, k_s, v_s)
```
